```python
import jax, jax.numpy as jnp
from jax import lax
import numpy as np

D_MODEL = 1024
BATCH = 8
SEQ = 8192
DEPTH = 2

HEAD_DIM = 64
D_ATTN = D_MODEL // 2
D_MLSTM = D_MODEL // 4
D_SGU = D_MODEL // 4
N_ATTN_HEADS = D_ATTN // HEAD_DIM
N_MLSTM_HEADS = D_MLSTM // HEAD_DIM
N_SGU_GROUPS = D_SGU // HEAD_DIM
SGU_GROUP_DIM = D_SGU // N_SGU_GROUPS
Q_BLOCK = 128
MLSTM_CHUNK = 64
SGU_CHUNK = 128
CONV_WIDTH = 3
D_FF = 11 * D_MODEL // 4
SPLITS = (D_ATTN, D_ATTN, D_ATTN, D_MLSTM, D_MLSTM, D_MLSTM, D_MLSTM, D_SGU, D_SGU, N_ATTN_HEADS, N_MLSTM_HEADS, N_MLSTM_HEADS)
N_IN = 3 * D_ATTN + 4 * D_MLSTM + 2 * D_SGU + N_ATTN_HEADS + 2 * N_MLSTM_HEADS
EPS = 1e-6

kernel_name = 'hybrid_fox_mlstm_sgu_block'


def rms_norm(x, g):
    xf = x.astype(jnp.float32)
    y = xf * lax.rsqrt(jnp.mean(xf * xf, axis=-1, keepdims=True) + EPS)
    return (y * g.astype(jnp.float32)).astype(x.dtype)


def layer_norm(x, g):
    xf = x.astype(jnp.float32)
    mu = jnp.mean(xf, axis=-1, keepdims=True)
    var = jnp.mean(jnp.square(xf - mu), axis=-1, keepdims=True)
    return ((xf - mu) * lax.rsqrt(var + EPS) * g.astype(jnp.float32)).astype(x.dtype)


def to_heads(t, n_heads):
    b, s, _ = t.shape
    return t.reshape(b, s, n_heads, -1)


def forgetting_attention(q, k, v, f_pre):
    s_len = q.shape[2]
    F = jnp.cumsum(jax.nn.log_sigmoid(f_pre.astype(jnp.float32)), axis=-1)
    scale = HEAD_DIM ** -0.5
    outs = []
    for blk in range(s_len // Q_BLOCK):
        lo, hi = blk * Q_BLOCK, (blk + 1) * Q_BLOCK
        qb = q[:, :, lo:hi]
        kb = k[:, :, :hi]
        vb = v[:, :, :hi]
        logits = (jnp.einsum('bhtd,bhsd->bhts', qb, kb).astype(jnp.float32) * scale
                  + F[:, :, lo:hi, None] - F[:, :, None, :hi])
        mask = (lo + jnp.arange(Q_BLOCK))[:, None] >= jnp.arange(hi)[None, :]
        logits = jnp.where(mask, logits, -jnp.inf)
        p = jax.nn.softmax(logits, axis=-1)
        outs.append(jnp.einsum('bhts,bhsd->bhtd', p.astype(v.dtype), vb))
    return jnp.concatenate(outs, axis=2)


def mlstm_chunkwise(q, k, v, i_pre, f_pre):
    out_dtype = v.dtype
    b, h, s_len, d = q.shape
    L = MLSTM_CHUNK
    n_chunks = s_len // L
    qf = q.astype(jnp.float32)
    kf = k.astype(jnp.float32) * (d ** -0.5)
    vf = v.astype(jnp.float32)
    ig = i_pre.astype(jnp.float32)
    logf = jax.nn.log_sigmoid(f_pre.astype(jnp.float32))

    def to_chunks(a):
        return jnp.moveaxis(a.reshape(b, h, n_chunks, L, *a.shape[3:]), 2, 0)

    causal = jnp.tril(jnp.ones((L, L), dtype=bool))

    def step(carry, xs):
        C, n, m = carry
        qb, kb, vb, ib, fb = xs
        bcum = jnp.cumsum(fb, axis=-1)
        log_d = bcum[..., :, None] - bcum[..., None, :] + ib[..., None, :]
        log_d = jnp.where(causal, log_d, -jnp.inf)
        log_inter = bcum + m[..., None]
        m_t = jnp.maximum(log_inter, jnp.max(log_d, axis=-1))
        w_intra = jnp.exp(log_d - m_t[..., None])
        w_inter = jnp.exp(log_inter - m_t)
        sc = jnp.einsum('bhtd,bhsd->bhts', qb, kb) * w_intra
        num = (w_inter[..., None] * jnp.einsum('bhtd,bhde->bhte', qb, C)
               + jnp.einsum('bhts,bhse->bhte', sc, vb))
        den = w_inter * jnp.einsum('bhtd,bhd->bht', qb, n) + jnp.sum(sc, axis=-1)
        h_out = num / jnp.maximum(jnp.abs(den), jnp.exp(-m_t))[..., None]
        b_last = bcum[..., -1]
        log_w_state = b_last[..., None] - bcum + ib
        m_new = jnp.maximum(b_last + m, jnp.max(log_w_state, axis=-1))
        w_state = jnp.exp(log_w_state - m_new[..., None])
        decay = jnp.exp(b_last + m - m_new)
        C_new = decay[..., None, None] * C + jnp.einsum('bhs,bhsd,bhse->bhde', w_state, kb, vb)
        n_new = decay[..., None] * n + jnp.einsum('bhs,bhsd->bhd', w_state, kb)
        return (C_new, n_new, m_new), h_out

    init = (jnp.zeros((b, h, d, d), jnp.float32),
            jnp.zeros((b, h, d), jnp.float32),
            jnp.zeros((b, h), jnp.float32))
    _, hs = lax.scan(step, init, (to_chunks(qf), to_chunks(kf), to_chunks(vf), to_chunks(ig), to_chunks(logf)))
    return jnp.moveaxis(hs, 0, 2).reshape(b, h, s_len, d).astype(out_dtype)


def spatial_gating(u, v, g, w_s, b_s):
    b, s_len, _ = u.shape
    n = s_len // SGU_CHUNK
    vn = layer_norm(v.reshape(b, s_len, N_SGU_GROUPS, SGU_GROUP_DIM), g.reshape(N_SGU_GROUPS, SGU_GROUP_DIM))
    vn = vn.reshape(b, n, SGU_CHUNK, N_SGU_GROUPS, SGU_GROUP_DIM)
    mask = jnp.tril(jnp.ones((SGU_CHUNK, SGU_CHUNK), dtype=bool))
    w = jnp.where(mask, w_s, 0.0).astype(vn.dtype)
    mixed = jnp.einsum('gts,bnsgc->bntgc', w, vn) + b_s.T[:, :, None].astype(vn.dtype)
    return u * mixed.reshape(b, s_len, D_SGU)


def causal_depthwise_conv(z, w, bias):
    ch = z.shape[-1]
    y = lax.conv_general_dilated(z, w[:, None, :].astype(z.dtype), window_strides=(1,),
                                 padding=[(CONV_WIDTH - 1, 0)],
                                 dimension_numbers=('NWC', 'WIO', 'NWC'),
                                 feature_group_count=ch)
    return y + bias.astype(z.dtype)


def conv_ffn(h, w_up, w_conv, b_conv, w_down):
    z = h @ w_up
    z = causal_depthwise_conv(z, w_conv, b_conv)
    a, gv = jnp.split(z, 2, axis=-1)
    return (jax.nn.silu(a) * gv) @ w_down


def hybrid_mixer(h, w_in, b_fox_f, b_m_i, b_m_f, g_q, g_k, g_m_out, g_sgu, w_s, b_s, w_out):
    b, s_len, _ = h.shape
    p = h @ w_in
    (q_a, k_a, v_a, q_m, k_m, v_m, o_m, u_s, v_s, f_a, i_m, f_m) = jnp.split(
        p, list(np.cumsum(SPLITS)[:-1]), axis=-1)
    qa = rms_norm(to_heads(q_a, N_ATTN_HEADS), g_q).transpose(0, 2, 1, 3)
    ka = rms_norm(to_heads(k_a, N_ATTN_HEADS), g_k).transpose(0, 2, 1, 3)
    va = to_heads(v_a, N_ATTN_HEADS).transpose(0, 2, 1, 3)
    fa = (f_a + b_fox_f).transpose(0, 2, 1)
    y_attn = forgetting_attention(qa, ka, va, fa).transpose(0, 2, 1, 3).reshape(b, s_len, D_ATTN)
    qm = to_heads(q_m, N_MLSTM_HEADS).transpose(0, 2, 1, 3)
    km = to_heads(k_m, N_MLSTM_HEADS).transpose(0, 2, 1, 3)
    vm = to_heads(v_m, N_MLSTM_HEADS).transpose(0, 2, 1, 3)
    im = (i_m + b_m_i).transpose(0, 2, 1)
    fm = (f_m + b_m_f).transpose(0, 2, 1)
    hm = mlstm_chunkwise(qm, km, vm, im, fm).transpose(0, 2, 1, 3)
    hm = rms_norm(hm, g_m_out.reshape(N_MLSTM_HEADS, HEAD_DIM)).reshape(b, s_len, D_MLSTM)
    y_mlstm = jax.nn.sigmoid(o_m) * hm
    y_sgu = spatial_gating(jax.nn.gelu(u_s), jax.nn.gelu(v_s), g_sgu, w_s, b_s)
    y = jnp.concatenate([y_attn, y_mlstm, y_sgu], axis=-1)
    return y @ w_out


def setup_inputs(seed: int = 0) -> dict:
    key = jax.random.key(seed)
    ks = jax.random.split(key, 21)

    def nrm(k, shape, scale):
        return jax.random.normal(k, shape, jnp.float32) * scale

    return {
        'x': nrm(ks[0], (BATCH, SEQ, D_MODEL), 1.0),
        'c': nrm(ks[1], (BATCH, D_MODEL), 1.0),
        'w_ada': nrm(ks[2], (DEPTH, D_MODEL, 6 * D_MODEL), D_MODEL ** -0.5),
        'b_ada': nrm(ks[3], (DEPTH, 6 * D_MODEL), 0.02),
        'g_mix': 1.0 + nrm(ks[4], (DEPTH, D_MODEL), 0.02),
        'g_ffn': 1.0 + nrm(ks[5], (DEPTH, D_MODEL), 0.02),
        'w_in': nrm(ks[6], (DEPTH, D_MODEL, N_IN), D_MODEL ** -0.5),
        'b_fox_f': 2.0 + nrm(ks[7], (DEPTH, N_ATTN_HEADS), 0.5),
        'b_m_i': nrm(ks[8], (DEPTH, N_MLSTM_HEADS), 0.1),
        'b_m_f': 3.0 + nrm(ks[9], (DEPTH, N_MLSTM_HEADS), 0.5),
        'g_q': 1.0 + nrm(ks[10], (DEPTH, HEAD_DIM), 0.02),
        'g_k': 1.0 + nrm(ks[11], (DEPTH, HEAD_DIM), 0.02),
        'g_m_out': 1.0 + nrm(ks[12], (DEPTH, D_MLSTM), 0.02),
        'g_sgu': 1.0 + nrm(ks[13], (DEPTH, D_SGU), 0.02),
        'w_s': nrm(ks[14], (DEPTH, N_SGU_GROUPS, SGU_CHUNK, SGU_CHUNK), SGU_CHUNK ** -0.5),
        'b_s': 1.0 + nrm(ks[15], (DEPTH, N_SGU_GROUPS, SGU_CHUNK), 0.1),
        'w_out': nrm(ks[16], (DEPTH, D_MODEL, D_MODEL), D_MODEL ** -0.5),
        'w_up': nrm(ks[17], (DEPTH, D_MODEL, 2 * D_FF), D_MODEL ** -0.5),
        'w_conv': nrm(ks[18], (DEPTH, CONV_WIDTH, 2 * D_FF), CONV_WIDTH ** -0.5),
        'b_conv': nrm(ks[19], (DEPTH, 2 * D_FF), 0.02),
        'w_down': nrm(ks[20], (DEPTH, D_FF, D_MODEL), D_FF ** -0.5),
    }


def reference(x, c, w_ada, b_ada, g_mix, g_ffn, w_in, b_fox_f, b_m_i, b_m_f, g_q, g_k,
              g_m_out, g_sgu, w_s, b_s, w_out, w_up, w_conv, b_conv, w_down):
    c_act = jax.nn.silu(c)
    for l in range(DEPTH):
        mod = c_act @ w_ada[l] + b_ada[l]
        sh1, sc1, gt1, sh2, sc2, gt2 = [m[:, None, :] for m in jnp.split(mod, 6, axis=-1)]
        h = rms_norm(x, g_mix[l]) * (1.0 + sc1) + sh1
        y = hybrid_mixer(h, w_in[l], b_fox_f[l], b_m_i[l], b_m_f[l], g_q[l], g_k[l],
                         g_m_out[l], g_sgu[l], w_s[l], b_s[l], w_out[l])
        x = x + gt1 * y
        h = rms_norm(x, g_ffn[l]) * (1.0 + sc2) + sh2
        x = x + gt2 * conv_ffn(h, w_up[l], w_conv[l], b_conv[l], w_down[l])
    return x
```

```python
import functools

import jax
import jax.numpy as jnp
from jax import lax
from jax.experimental import pallas as pl
from jax.experimental.pallas import tpu as pltpu

F32 = jnp.float32
BF16 = jnp.bfloat16

D_MODEL = 1024
HEAD_DIM = 64
D_ATTN = 512
D_MLSTM = 256
D_SGU = 256
N_ATTN_HEADS = 8
N_MLSTM_HEADS = 4
N_SGU_GROUPS = 4
SGU_CHUNK = 128
D_FF = 2816
EPS = 1e-6
LOG2E = 1.4426950408889634

LANES = 128
IN_TILE = 512
ATTN_TILE = 256
MLSTM_CHUNK = 256
POST_TILE = 512
FF_CHUNK = 256
GATE_LANE_M = 8
NEG_BIG = -1e30
VMEM_LIMIT = 56 * 1024 * 1024


def _nt_dot(a, b):
    return lax.dot_general(a, b, (((1,), (1,)), ((), ())), preferred_element_type=F32)


def _tn_dot(a, b):
    return lax.dot_general(a, b, (((0,), (0,)), ((), ())), preferred_element_type=F32)


def _dot(a, b):
    return jnp.dot(a, b, preferred_element_type=F32)


def _split_dot(x, e):
    hi = x.astype(BF16)
    lo = (x - hi.astype(F32)).astype(BF16)
    return _dot(hi, e) + _dot(lo, e)


def _scan_rows(x, op, identity):
    n = x.shape[0]
    row = lax.broadcasted_iota(jnp.int32, x.shape, 0)
    k = 1
    while k < n:
        shifted = pltpu.roll(x, k, axis=0)
        x = op(x, jnp.where(row >= k, shifted, identity))
        k *= 2
    return x


def _mod_kernel(c_ref, w_ref, b_ref, o_ref):
    c = c_ref[...]
    ca = c * jax.nn.sigmoid(c)
    o_ref[0] = jnp.dot(ca, w_ref[0], preferred_element_type=F32,
                       precision=lax.Precision.HIGHEST) + b_ref[0]


def _modulation(c, w_ada, b_ada):
    depth, d, n = w_ada.shape
    b = c.shape[0]
    tn = 1536
    return pl.pallas_call(
        _mod_kernel,
        grid=(depth, n // tn),
        in_specs=[
            pl.BlockSpec((b, d), lambda l, j: (0, 0)),
            pl.BlockSpec((1, d, tn), lambda l, j: (l, 0, j)),
            pl.BlockSpec((1, 1, tn), lambda l, j: (l, 0, j)),
        ],
        out_specs=pl.BlockSpec((1, b, tn), lambda l, j: (l, 0, j)),
        out_shape=jax.ShapeDtypeStruct((depth, b, n), F32),
        compiler_params=pltpu.CompilerParams(
            dimension_semantics=("arbitrary", "arbitrary"), vmem_limit_bytes=VMEM_LIMIT),
        name="modulation",
    )(c, w_ada, b_ada.reshape(depth, 1, n))


def _in_kernel(x_ref, sh_ref, sc_ref, g_ref, wqT_ref, wk_ref, wvT_ref, wm_ref, ws_ref, wg_ref,
               gq_ref, gk_ref, e512_ref, e256_ref, gsgu_ref, wst_ref, bst_ref, gbias_ref,
               qT_ref, k_ref, vT_ref, qkvm_ref, osig_ref, ysgu_ref, gn_ref, gT_ref,
               carry_ref):
    tm = x_ref.shape[1]

    @pl.when(pl.program_id(1) == 0)
    def _():
        carry_ref[...] = jnp.zeros_like(carry_ref)

    x = x_ref[0]
    ms = jnp.mean(x * x, axis=-1, keepdims=True)
    h = x * lax.rsqrt(ms + EPS) * g_ref[...] * (1.0 + sc_ref[0]) + sh_ref[0]
    hb = h.astype(BF16)

    qT = _nt_dot(wqT_ref[...], hb)
    q_scale = (HEAD_DIM ** -0.5) * LOG2E
    for hd in range(N_ATTN_HEADS):
        blk = qT[hd * HEAD_DIM:(hd + 1) * HEAD_DIM]
        msq = jnp.mean(blk * blk, axis=0, keepdims=True)
        qn = blk * lax.rsqrt(msq + EPS) * (gq_ref[...] * q_scale)
        qT_ref[0, hd * HEAD_DIM:(hd + 1) * HEAD_DIM, :] = qn.astype(BF16)

    kk = _dot(hb, wk_ref[...])
    msk = _dot((kk * kk).astype(BF16), e512_ref[...])
    k_ref[0] = (kk * lax.rsqrt(msk + EPS) * gk_ref[...]).astype(BF16)

    vT_ref[0] = _nt_dot(wvT_ref[...], hb).astype(BF16)

    pm = _dot(hb, wm_ref[...])
    qkvm_ref[0, :, 0:D_MLSTM] = pm[:, 0:D_MLSTM].astype(BF16)
    qkvm_ref[0, :, D_MLSTM:2 * D_MLSTM] = (pm[:, D_MLSTM:2 * D_MLSTM] * HEAD_DIM ** -0.5).astype(BF16)
    qkvm_ref[0, :, 2 * D_MLSTM:3 * D_MLSTM] = pm[:, 2 * D_MLSTM:3 * D_MLSTM].astype(BF16)
    osig_ref[0] = jax.nn.sigmoid(pm[:, 3 * D_MLSTM:4 * D_MLSTM])

    pg = _dot(hb, wg_ref[...]) + gbias_ref[...]
    lane = lax.broadcasted_iota(jnp.int32, (tm, LANES), 1)
    ls = jnp.where(lane < GATE_LANE_M + N_MLSTM_HEADS, jax.nn.log_sigmoid(pg[:, 0:LANES]), 0.0)
    cs = _scan_rows(ls, jnp.add, 0.0) + carry_ref[0:1, :]
    a = pg[:, LANES:2 * LANES] - cs
    cm = jnp.maximum(_scan_rows(a, jnp.maximum, -jnp.inf), carry_ref[8:9, :])
    carry_ref[0:1, :] = cs[tm - 1:tm, :]
    carry_ref[8:9, :] = cm[tm - 1:tm, :]
    g1 = jnp.where(lane < GATE_LANE_M, -LOG2E * cs, a)
    gn_ref[0, :, 0:LANES] = g1
    gn_ref[0, :, LANES:2 * LANES] = cm
    gn_ref[0, :, 2 * LANES:3 * LANES] = cs + cm
    gT_ref[0] = g1.T

    ps = _dot(hb, ws_ref[...])
    gu = jax.nn.gelu(ps[:, 0:D_SGU])
    gv = jax.nn.gelu(ps[:, D_SGU:2 * D_SGU])
    e256 = e256_ref[...]
    xc = gv - _split_dot(gv, e256)
    var = _split_dot(xc * xc, e256)
    vnb = (xc * lax.rsqrt(var + EPS) * gsgu_ref[...]).astype(BF16)
    nrow = N_SGU_GROUPS * SGU_CHUNK
    wrow = lax.broadcasted_iota(jnp.int32, (nrow, SGU_CHUNK), 0)
    wcol = lax.broadcasted_iota(jnp.int32, (nrow, SGU_CHUNK), 1)
    wst = jnp.where((wrow % SGU_CHUNK) >= wcol, wst_ref[...], 0.0).astype(BF16)
    grp = lax.broadcasted_iota(jnp.int32, (SGU_CHUNK, D_SGU), 1) // HEAD_DIM
    for c in range(tm // SGU_CHUNK):
        r = _dot(wst, vnb[c * SGU_CHUNK:(c + 1) * SGU_CHUNK])
        mixed = r[0:SGU_CHUNK]
        for gidx in range(1, N_SGU_GROUPS):
            mixed = jnp.where(grp == gidx, r[gidx * SGU_CHUNK:(gidx + 1) * SGU_CHUNK], mixed)
        ysgu_ref[0, c * SGU_CHUNK:(c + 1) * SGU_CHUNK, :] = (
            gu[c * SGU_CHUNK:(c + 1) * SGU_CHUNK] * (mixed + bst_ref[...])).astype(BF16)


def _const_spec(shape):
    nd = len(shape)
    return pl.BlockSpec(shape, lambda b, s: (0,) * nd, pipeline_mode=pl.Buffered(1))


def _in_projection(x, sh, sc, g, w):
    b, s, d = x.shape
    tm = IN_TILE
    tok = lambda width: pl.BlockSpec((1, tm, width), lambda bi, si: (bi, si, 0))
    tokT = lambda rows: pl.BlockSpec((1, rows, tm), lambda bi, si: (bi, 0, si))
    mod = pl.BlockSpec((1, 1, d), lambda bi, si: (bi, 0, 0))
    consts = [g, w["wqT"], w["wk"], w["wvT"], w["wm"], w["ws"], w["wg"], w["gq"], w["gk"],
              w["e512"], w["e256"], w["gsgu"], w["wst"], w["bst"], w["gbias"]]
    return pl.pallas_call(
        _in_kernel,
        grid=(b, s // tm),
        in_specs=[tok(d), mod, mod] + [_const_spec(c.shape) for c in consts],
        out_specs=[tokT(D_ATTN), tok(D_ATTN), tokT(D_ATTN), tok(3 * D_MLSTM), tok(D_MLSTM),
                   tok(D_SGU), tok(3 * LANES), tokT(LANES)],
        out_shape=[
            jax.ShapeDtypeStruct((b, D_ATTN, s), BF16),
            jax.ShapeDtypeStruct((b, s, D_ATTN), BF16),
            jax.ShapeDtypeStruct((b, D_ATTN, s), BF16),
            jax.ShapeDtypeStruct((b, s, 3 * D_MLSTM), BF16),
            jax.ShapeDtypeStruct((b, s, D_MLSTM), F32),
            jax.ShapeDtypeStruct((b, s, D_SGU), BF16),
            jax.ShapeDtypeStruct((b, s, 3 * LANES), F32),
            jax.ShapeDtypeStruct((b, LANES, s), F32),
        ],
        scratch_shapes=[pltpu.VMEM((16, LANES), F32)],
        compiler_params=pltpu.CompilerParams(
            dimension_semantics=("arbitrary", "arbitrary"), vmem_limit_bytes=VMEM_LIMIT),
        name="in_projection",
    )(x, sh, sc, *consts)


def _attn_kernel(qT_ref, k_ref, vT_ref, g_ref, o_ref, nfb_ref):
    s_len = k_ref.shape[1]
    tq = qT_ref.shape[2]
    tk = tq
    pair = pl.program_id(1)
    qi = pl.program_id(2)

    @pl.when(qi == 0)
    def _():
        lane = lax.broadcasted_iota(jnp.int32, (tk, LANES), 1)

        def fill(c, carry):
            off = pl.multiple_of(c * tk, tk)
            g = g_ref[0, pl.ds(off, tk), :]
            for j in range(2):
                col = jnp.sum(jnp.where(lane == 2 * pair + j, g, 0.0), axis=1, keepdims=True)
                nfb_ref[j, pl.ds(off, tk), :] = jnp.broadcast_to(col, (tk, LANES))
            return carry

        lax.fori_loop(0, s_len // tk, fill, 0)

    qT = qT_ref[0]
    row = lax.broadcasted_iota(jnp.int32, qT.shape, 0)
    zero = jnp.zeros_like(qT)
    qpad = (jnp.where(row < HEAD_DIM, qT, zero), jnp.where(row >= HEAD_DIM, qT, zero))

    def step(kb, carry, masked):
        off = pl.multiple_of(kb * tk, tk)
        ks = k_ref[0, pl.ds(off, tk), :]
        out = []
        for j in range(2):
            m, l, acc = carry[j]
            s = _dot(ks, qpad[j])
            nf = nfb_ref[j, pl.ds(off, tk), :]
            s = s + jnp.concatenate([nf] * (tq // LANES), axis=1)
            if masked:
                kidx = lax.broadcasted_iota(jnp.int32, s.shape, 0)
                qidx = lax.broadcasted_iota(jnp.int32, s.shape, 1)
                s = jnp.where(kidx <= qidx, s, NEG_BIG)
            m_new = jnp.maximum(m, jnp.max(s, axis=0, keepdims=True))
            alpha = jnp.exp2(m - m_new)
            p = jnp.exp2(s - m_new)
            l = alpha * l + jnp.sum(p, axis=0, keepdims=True)
            vt = vT_ref[0, j * HEAD_DIM:(j + 1) * HEAD_DIM, pl.ds(off, tk)]
            acc = alpha * acc + _dot(vt, p.astype(BF16))
            out.append((m_new, l, acc))
        return tuple(out)

    init = tuple((jnp.full((1, tq), NEG_BIG, F32), jnp.zeros((1, tq), F32),
                  jnp.zeros((HEAD_DIM, tq), F32)) for _ in range(2))
    carry = lax.fori_loop(0, qi, functools.partial(step, masked=False), init)
    carry = step(qi, carry, masked=True)
    outT = jnp.concatenate([carry[j][2] / carry[j][1] for j in range(2)], axis=0)
    o_ref[0] = outT.T.astype(BF16)


def _fox_attention(qT, k, vT, gn):
    b, s, _ = k.shape
    tq = ATTN_TILE
    pairs = N_ATTN_HEADS // 2
    return pl.pallas_call(
        _attn_kernel,
        grid=(b, pairs, s // tq),
        in_specs=[
            pl.BlockSpec((1, 2 * HEAD_DIM, tq), lambda bi, p, qi: (bi, p, qi)),
            pl.BlockSpec((1, s, 2 * HEAD_DIM), lambda bi, p, qi: (bi, 0, p)),
            pl.BlockSpec((1, 2 * HEAD_DIM, s), lambda bi, p, qi: (bi, p, 0)),
            pl.BlockSpec((1, s, LANES), lambda bi, p, qi: (bi, 0, 0)),
        ],
        out_specs=pl.BlockSpec((1, tq, 2 * HEAD_DIM), lambda bi, p, qi: (bi, qi, p)),
        out_shape=jax.ShapeDtypeStruct((b, s, D_ATTN), BF16),
        scratch_shapes=[pltpu.VMEM((2, s, LANES), F32)],
        compiler_params=pltpu.CompilerParams(
            dimension_semantics=("arbitrary", "arbitrary", "arbitrary"),
            vmem_limit_bytes=VMEM_LIMIT),
        name="fox_attention",
    )(qT, k, vT, gn)


def _mlstm_kernel(qkv_ref, osig_ref, gn_ref, gT_ref, e256_ref, gmo_ref, o_ref, cn_ref, acs_ref):
    L = qkv_ref.shape[1]
    dm = D_MLSTM

    @pl.when(pl.program_id(1) == 0)
    def _():
        cn_ref[...] = jnp.zeros_like(cn_ref)
        acs_ref[...] = jnp.zeros_like(acs_ref)

    q = qkv_ref[0, :, 0:dm]
    k = qkv_ref[0, :, dm:2 * dm]
    v = qkv_ref[0, :, 2 * dm:3 * dm]
    a_nat = gn_ref[0, :, 0:LANES]
    A_nat = gn_ref[0, :, LANES:2 * LANES]
    m_nat = gn_ref[0, :, 2 * LANES:3 * LANES]
    aT = gT_ref[0]
    A_prev = acs_ref[0:1, :]
    A_end = A_nat[L - 1:L, :]

    grp = lax.broadcasted_iota(jnp.int32, (L, dm), 1) // HEAD_DIM
    causal = (lax.broadcasted_iota(jnp.int32, (L, L), 0) >= lax.broadcasted_iota(jnp.int32, (L, L), 1))
    qcn = _dot(q, cn_ref[...].astype(BF16))
    zero_q = jnp.zeros_like(q)
    hout = jnp.zeros((L, dm), F32)
    wst = jnp.zeros((L, dm), F32)
    for hd in range(N_MLSTM_HEADS):
        gl = GATE_LANE_M + hd
        a_row = aT[gl:gl + 1, :]
        A_col = A_nat[:, gl:gl + 1]
        w = jnp.exp(jnp.where(causal, a_row - A_col, -jnp.inf))
        sw = _nt_dot(jnp.where(grp == hd, q, zero_q), k) * w
        den_intra = jnp.sum(sw, axis=1, keepdims=True)
        pv = _dot(sw.astype(BF16), v)
        e_col = jnp.exp(A_prev[:, gl:gl + 1] - A_col)
        den = den_intra + e_col * qcn[:, dm + hd:dm + hd + 1]
        r = 1.0 / jnp.maximum(jnp.abs(den), jnp.exp(-m_nat[:, gl:gl + 1]))
        hout = jnp.where(grp == hd, (pv + e_col * qcn[:, 0:dm]) * r, hout)
        wst = jnp.where(grp == hd, jnp.exp(a_nat[:, gl:gl + 1] - A_end[:, gl:gl + 1]), wst)

    msq = _split_dot(hout * hout, e256_ref[...])
    hn = hout * lax.rsqrt(msq + EPS) * gmo_ref[...]
    o_ref[0] = (osig_ref[0] * hn).astype(BF16)

    kw = (k.astype(F32) * wst).astype(BF16)
    rhs = jnp.concatenate([v, jnp.ones((L, LANES), BF16)], axis=1)
    upd = _tn_dot(kw, rhs)
    rowg = lax.broadcasted_iota(jnp.int32, upd.shape, 0) // HEAD_DIM
    col = lax.broadcasted_iota(jnp.int32, upd.shape, 1)
    keep = jnp.where(col < dm, col // HEAD_DIM, col - dm) == rowg
    dvec = jnp.exp(A_prev - A_end)
    decay = jnp.zeros(upd.shape, F32)
    for hd in range(N_MLSTM_HEADS):
        gl = GATE_LANE_M + hd
        decay = jnp.where(rowg == hd, dvec[:, gl:gl + 1], decay)
    cn_ref[...] = decay * cn_ref[...] + jnp.where(keep, upd, 0.0)
    acs_ref[0:1, :] = A_end


def _mlstm(qkvm, osig, gn, gT, e256, gmo):
    b, s, _ = qkvm.shape
    L = MLSTM_CHUNK
    return pl.pallas_call(
        _mlstm_kernel,
        grid=(b, s // L),
        in_specs=[
            pl.BlockSpec((1, L, 3 * D_MLSTM), lambda bi, ci: (bi, ci, 0)),
            pl.BlockSpec((1, L, D_MLSTM), lambda bi, ci: (bi, ci, 0)),
            pl.BlockSpec((1, L, 3 * LANES), lambda bi, ci: (bi, ci, 0)),
            pl.BlockSpec((1, LANES, L), lambda bi, ci: (bi, 0, ci)),
            _const_spec(e256.shape),
            _const_spec(gmo.shape),
        ],
        out_specs=pl.BlockSpec((1, L, D_MLSTM), lambda bi, ci: (bi, ci, 0)),
        out_shape=jax.ShapeDtypeStruct((b, s, D_MLSTM), BF16),
        scratch_shapes=[pltpu.VMEM((D_MLSTM, D_MLSTM + LANES), F32), pltpu.VMEM((8, LANES), F32)],
        compiler_params=pltpu.CompilerParams(
            dimension_semantics=("arbitrary", "arbitrary"), vmem_limit_bytes=VMEM_LIMIT),
        name="mlstm",
    )(qkvm, osig, gn, gT, e256, gmo)


def _post_kernel(x_ref, ya_ref, ym_ref, ys_ref, gt1_ref, sh2_ref, sc2_ref, gt2_ref, g_ref,
                 wo_ref, wup_ref, wconv_ref, bconv_ref, wdown_ref, o_ref, halo_ref, hact_ref):
    tm = x_ref.shape[1]

    @pl.when(pl.program_id(1) == 0)
    def _():
        halo_ref[...] = jnp.zeros_like(halo_ref)

    y = (_dot(ya_ref[0], wo_ref[0:D_ATTN, :])
         + _dot(ym_ref[0], wo_ref[D_ATTN:D_ATTN + D_MLSTM, :])
         + _dot(ys_ref[0], wo_ref[D_ATTN + D_MLSTM:D_MODEL, :]))
    x1 = x_ref[0] + gt1_ref[0] * y
    ms = jnp.mean(x1 * x1, axis=-1, keepdims=True)
    h2 = (x1 * lax.rsqrt(ms + EPS) * g_ref[...] * (1.0 + sc2_ref[0]) + sh2_ref[0]).astype(BF16)

    row = lax.broadcasted_iota(jnp.int32, (tm, FF_CHUNK), 0)

    def conv(cols):
        z = _dot(h2, wup_ref[:, cols])
        prev2 = halo_ref[6:7, cols]
        prev1 = halo_ref[7:8, cols]
        halo_ref[:, cols] = z[tm - 8:tm, :]
        z1 = jnp.where(row == 0, prev1, pltpu.roll(z, 1, axis=0))
        z2 = jnp.where(row == 0, prev2, jnp.where(row == 1, prev1, pltpu.roll(z, 2, axis=0)))
        return (wconv_ref[0:1, cols] * z2 + wconv_ref[1:2, cols] * z1 + wconv_ref[2:3, cols] * z
                + bconv_ref[:, cols])

    for c in range(D_FF // FF_CHUNK):
        a = conv(pl.ds(c * FF_CHUNK, FF_CHUNK))
        gv = conv(pl.ds(D_FF + c * FF_CHUNK, FF_CHUNK))
        hact_ref[:, c * FF_CHUNK:(c + 1) * FF_CHUNK] = (a * jax.nn.sigmoid(a) * gv).astype(BF16)

    o_ref[0] = x1 + gt2_ref[0] * _dot(hact_ref[...], wdown_ref[...])


def _post(x, ya, ym, ys, gt1, sh2, sc2, gt2, g, wo, wup, wconv, bconv, wdown):
    b, s, d = x.shape
    tm = POST_TILE
    tok = lambda width: pl.BlockSpec((1, tm, width), lambda bi, si: (bi, si, 0))
    mod = pl.BlockSpec((1, 1, d), lambda bi, si: (bi, 0, 0))
    consts = [g, wo, wup, wconv, bconv, wdown]
    return pl.pallas_call(
        _post_kernel,
        grid=(b, s // tm),
        in_specs=[tok(d), tok(D_ATTN), tok(D_MLSTM), tok(D_SGU), mod, mod, mod, mod]
        + [_const_spec(c.shape) for c in consts],
        out_specs=tok(d),
        out_shape=jax.ShapeDtypeStruct((b, s, d), F32),
        scratch_shapes=[pltpu.VMEM((8, 2 * D_FF), F32), pltpu.VMEM((tm, D_FF), BF16)],
        compiler_params=pltpu.CompilerParams(
            dimension_semantics=("arbitrary", "arbitrary"), vmem_limit_bytes=VMEM_LIMIT),
        name="out_proj_ffn",
    )(x, ya, ym, ys, gt1, sh2, sc2, gt2, *consts)


def _block_diag_mean(n):
    idx = jnp.arange(n) // HEAD_DIM
    return jnp.where(idx[:, None] == idx[None, :], 1.0 / HEAD_DIM, 0.0).astype(BF16)


def _prep_layer(w_in, b_fox_f, b_m_i, b_m_f, g_q, g_k, g_m_out, g_sgu, w_s, b_s):
    o = 0
    wq = w_in[:, o:o + D_ATTN]; o += D_ATTN
    wk = w_in[:, o:o + D_ATTN]; o += D_ATTN
    wv = w_in[:, o:o + D_ATTN]; o += D_ATTN
    wm = w_in[:, o:o + 4 * D_MLSTM]; o += 4 * D_MLSTM
    ws = w_in[:, o:o + 2 * D_SGU]; o += 2 * D_SGU
    wfa = w_in[:, o:o + N_ATTN_HEADS]; o += N_ATTN_HEADS
    wim = w_in[:, o:o + N_MLSTM_HEADS]; o += N_MLSTM_HEADS
    wfm = w_in[:, o:o + N_MLSTM_HEADS]
    lo, hi = GATE_LANE_M, GATE_LANE_M + N_MLSTM_HEADS
    wg = jnp.zeros((D_MODEL, 2 * LANES), F32)
    wg = wg.at[:, 0:N_ATTN_HEADS].set(wfa).at[:, lo:hi].set(wfm).at[:, LANES + lo:LANES + hi].set(wim)
    gbias = jnp.zeros((1, 2 * LANES), F32)
    gbias = (gbias.at[0, 0:N_ATTN_HEADS].set(b_fox_f).at[0, lo:hi].set(b_m_f)
             .at[0, LANES + lo:LANES + hi].set(b_m_i))
    return dict(
        wqT=wq.T.astype(BF16), wk=wk.astype(BF16), wvT=wv.T.astype(BF16), wm=wm.astype(BF16),
        ws=ws.astype(BF16), wg=wg.astype(BF16), gbias=gbias,
        gq=g_q.reshape(HEAD_DIM, 1), gk=jnp.tile(g_k, N_ATTN_HEADS).reshape(1, D_ATTN),
        e512=_block_diag_mean(D_ATTN), e256=_block_diag_mean(D_SGU),
        gsgu=g_sgu.reshape(1, D_SGU), wst=w_s.reshape(N_SGU_GROUPS * SGU_CHUNK, SGU_CHUNK),
        bst=jnp.repeat(b_s.T, HEAD_DIM, axis=1), gmo=g_m_out.reshape(1, D_MLSTM),
    )


def kernel(x, c, w_ada, b_ada, g_mix, g_ffn, w_in, b_fox_f, b_m_i, b_m_f, g_q, g_k, g_m_out, g_sgu, w_s, b_s, w_out, w_up, w_conv, b_conv, w_down):
    depth = w_ada.shape[0]
    b, s, d = x.shape
    assert d == D_MODEL and s % max(IN_TILE, POST_TILE, MLSTM_CHUNK, ATTN_TILE) == 0
    mod = _modulation(c, w_ada, b_ada)
    for l in range(depth):
        sh1, sc1, gt1, sh2, sc2, gt2 = [mod[l, :, i * d:(i + 1) * d].reshape(b, 1, d) for i in range(6)]
        w = _prep_layer(w_in[l], b_fox_f[l], b_m_i[l], b_m_f[l], g_q[l], g_k[l], g_m_out[l],
                        g_sgu[l], w_s[l], b_s[l])
        qT, k, vT, qkvm, osig, ysgu, gn, gT = _in_projection(x, sh1, sc1, g_mix[l].reshape(1, d), w)
        ya = _fox_attention(qT, k, vT, gn)
        ym = _mlstm(qkvm, osig, gn, gT, w["e256"], w["gmo"])
        x = _post(x, ya, ym, ysgu, gt1, sh2, sc2, gt2, g_ffn[l].reshape(1, d),
                  w_out[l].astype(BF16), w_up[l].astype(BF16), w_conv[l],
                  b_conv[l].reshape(1, 2 * D_FF), w_down[l].astype(BF16))
    return x
```

```python
import functools

import jax
import jax.numpy as jnp
from jax import lax
from jax.experimental import pallas as pl
from jax.experimental.pallas import tpu as pltpu

F32 = jnp.float32
BF16 = jnp.bfloat16

D_MODEL = 1024
HEAD_DIM = 64
D_ATTN = 512
D_MLSTM = 256
D_SGU = 256
N_ATTN_HEADS = 8
N_MLSTM_HEADS = 4
N_SGU_GROUPS = 4
SGU_CHUNK = 128
D_FF = 2816
EPS = 1e-6
LOG2E = 1.4426950408889634

LANES = 128
IN_TILE = 512
ATTN_TILE = 512
BIAS_PARTS = 3
ONES_ROWS = 16
MLSTM_CHUNK = 256
POST_TILE = 512
FF_CHUNK = 256
GATE_LANE_M = 8
NEG_BIG = -1e30
VMEM_LIMIT = 56 * 1024 * 1024


def _nt_dot(a, b):
    return lax.dot_general(a, b, (((1,), (1,)), ((), ())), preferred_element_type=F32)


def _tn_dot(a, b):
    return lax.dot_general(a, b, (((0,), (0,)), ((), ())), preferred_element_type=F32)


def _dot(a, b):
    return jnp.dot(a, b, preferred_element_type=F32)


def _split_dot(x, e):
    hi = x.astype(BF16)
    lo = (x - hi.astype(F32)).astype(BF16)
    return _dot(hi, e) + _dot(lo, e)


def _scan_rows(x, op, identity):
    n = x.shape[0]
    row = lax.broadcasted_iota(jnp.int32, x.shape, 0)
    k = 1
    while k < n:
        shifted = pltpu.roll(x, k, axis=0)
        x = op(x, jnp.where(row >= k, shifted, identity))
        k *= 2
    return x


def _mod_kernel(c_ref, w_ref, b_ref, o_ref):
    c = c_ref[...]
    ca = c * jax.nn.sigmoid(c)
    o_ref[0] = jnp.dot(ca, w_ref[0], preferred_element_type=F32,
                       precision=lax.Precision.HIGHEST) + b_ref[0]


def _modulation(c, w_ada, b_ada):
    depth, d, n = w_ada.shape
    b = c.shape[0]
    tn = 1536
    return pl.pallas_call(
        _mod_kernel,
        grid=(depth, n // tn),
        in_specs=[
            pl.BlockSpec((b, d), lambda l, j: (0, 0)),
            pl.BlockSpec((1, d, tn), lambda l, j: (l, 0, j)),
            pl.BlockSpec((1, 1, tn), lambda l, j: (l, 0, j)),
        ],
        out_specs=pl.BlockSpec((1, b, tn), lambda l, j: (l, 0, j)),
        out_shape=jax.ShapeDtypeStruct((depth, b, n), F32),
        compiler_params=pltpu.CompilerParams(
            dimension_semantics=("arbitrary", "arbitrary"), vmem_limit_bytes=VMEM_LIMIT),
        name="modulation",
    )(c, w_ada, b_ada.reshape(depth, 1, n))


def _in_kernel(x_ref, sh_ref, sc_ref, g_ref, wqT_ref, wk_ref, wvT_ref, wm_ref, ws_ref, wg_ref,
               gq_ref, gk_ref, e512_ref, e256_ref, gsgu_ref, wst_ref, bst_ref, gbias_ref, place_ref,
               qT_ref, k_ref, kb_ref, vT_ref, qkvm_ref, osig_ref, ysgu_ref, gn_ref, gT_ref,
               carry_ref):
    tm = x_ref.shape[1]

    @pl.when(pl.program_id(1) == 0)
    def _():
        carry_ref[...] = jnp.zeros_like(carry_ref)

    x = x_ref[0]
    ms = jnp.mean(x * x, axis=-1, keepdims=True)
    h = x * lax.rsqrt(ms + EPS) * g_ref[...] * (1.0 + sc_ref[0]) + sh_ref[0]
    hb = h.astype(BF16)

    qT = _nt_dot(wqT_ref[...], hb)
    q_scale = (HEAD_DIM ** -0.5) * LOG2E
    for hd in range(N_ATTN_HEADS):
        blk = qT[hd * HEAD_DIM:(hd + 1) * HEAD_DIM]
        msq = jnp.mean(blk * blk, axis=0, keepdims=True)
        qn = blk * lax.rsqrt(msq + EPS) * (gq_ref[...] * q_scale)
        qT_ref[0, hd * HEAD_DIM:(hd + 1) * HEAD_DIM, :] = qn.astype(BF16)

    kk = _dot(hb, wk_ref[...])
    msk = _dot((kk * kk).astype(BF16), e512_ref[...])
    k_ref[0] = (kk * lax.rsqrt(msk + EPS) * gk_ref[...]).astype(BF16)

    vT_ref[0] = _nt_dot(wvT_ref[...], hb).astype(BF16)

    pm = _dot(hb, wm_ref[...])
    qkvm_ref[0, :, 0:D_MLSTM] = pm[:, 0:D_MLSTM].astype(BF16)
    qkvm_ref[0, :, D_MLSTM:2 * D_MLSTM] = (pm[:, D_MLSTM:2 * D_MLSTM] * HEAD_DIM ** -0.5).astype(BF16)
    qkvm_ref[0, :, 2 * D_MLSTM:3 * D_MLSTM] = pm[:, 2 * D_MLSTM:3 * D_MLSTM].astype(BF16)
    osig_ref[0] = jax.nn.sigmoid(pm[:, 3 * D_MLSTM:4 * D_MLSTM])

    pg = _dot(hb, wg_ref[...]) + gbias_ref[...]
    lane = lax.broadcasted_iota(jnp.int32, (tm, LANES), 1)
    ls = jnp.where(lane < GATE_LANE_M + N_MLSTM_HEADS, jax.nn.log_sigmoid(pg[:, 0:LANES]), 0.0)
    cs = _scan_rows(ls, jnp.add, 0.0) + carry_ref[0:1, :]
    a = pg[:, LANES:2 * LANES] - cs
    cm = jnp.maximum(_scan_rows(a, jnp.maximum, -jnp.inf), carry_ref[8:9, :])
    carry_ref[0:1, :] = cs[tm - 1:tm, :]
    carry_ref[8:9, :] = cm[tm - 1:tm, :]
    nf = -LOG2E * cs
    hi = nf.astype(BF16)
    r1 = nf - hi.astype(F32)
    mid = r1.astype(BF16)
    lo = (r1 - mid.astype(F32)).astype(BF16)
    kb_ref[0] = _dot(jnp.concatenate([hi, mid, lo], axis=1), place_ref[...]).astype(BF16)
    g1 = jnp.where(lane < GATE_LANE_M, nf, a)
    gn_ref[0, :, 0:LANES] = g1
    gn_ref[0, :, LANES:2 * LANES] = cm
    gn_ref[0, :, 2 * LANES:3 * LANES] = cs + cm
    gT_ref[0] = g1.T

    ps = _dot(hb, ws_ref[...])
    gu = jax.nn.gelu(ps[:, 0:D_SGU])
    gv = jax.nn.gelu(ps[:, D_SGU:2 * D_SGU])
    e256 = e256_ref[...]
    xc = gv - _split_dot(gv, e256)
    var = _split_dot(xc * xc, e256)
    vnb = (xc * lax.rsqrt(var + EPS) * gsgu_ref[...]).astype(BF16)
    nrow = N_SGU_GROUPS * SGU_CHUNK
    wrow = lax.broadcasted_iota(jnp.int32, (nrow, SGU_CHUNK), 0)
    wcol = lax.broadcasted_iota(jnp.int32, (nrow, SGU_CHUNK), 1)
    wst = jnp.where((wrow % SGU_CHUNK) >= wcol, wst_ref[...], 0.0).astype(BF16)
    grp = lax.broadcasted_iota(jnp.int32, (SGU_CHUNK, D_SGU), 1) // HEAD_DIM
    for c in range(tm // SGU_CHUNK):
        r = _dot(wst, vnb[c * SGU_CHUNK:(c + 1) * SGU_CHUNK])
        mixed = r[0:SGU_CHUNK]
        for gidx in range(1, N_SGU_GROUPS):
            mixed = jnp.where(grp == gidx, r[gidx * SGU_CHUNK:(gidx + 1) * SGU_CHUNK], mixed)
        ysgu_ref[0, c * SGU_CHUNK:(c + 1) * SGU_CHUNK, :] = (
            gu[c * SGU_CHUNK:(c + 1) * SGU_CHUNK] * (mixed + bst_ref[...])).astype(BF16)


def _const_spec(shape):
    nd = len(shape)
    return pl.BlockSpec(shape, lambda b, s: (0,) * nd, pipeline_mode=pl.Buffered(1))


def _in_projection(x, sh, sc, g, w):
    b, s, d = x.shape
    tm = IN_TILE
    tok = lambda width: pl.BlockSpec((1, tm, width), lambda bi, si: (bi, si, 0))
    tokT = lambda rows: pl.BlockSpec((1, rows, tm), lambda bi, si: (bi, 0, si))
    mod = pl.BlockSpec((1, 1, d), lambda bi, si: (bi, 0, 0))
    consts = [g, w["wqT"], w["wk"], w["wvT"], w["wm"], w["ws"], w["wg"], w["gq"], w["gk"],
              w["e512"], w["e256"], w["gsgu"], w["wst"], w["bst"], w["gbias"], w["place"]]
    return pl.pallas_call(
        _in_kernel,
        grid=(b, s // tm),
        in_specs=[tok(d), mod, mod] + [_const_spec(c.shape) for c in consts],
        out_specs=[tokT(D_ATTN), tok(D_ATTN), tok(D_ATTN), tokT(D_ATTN), tok(3 * D_MLSTM),
                   tok(D_MLSTM), tok(D_SGU), tok(3 * LANES), tokT(LANES)],
        out_shape=[
            jax.ShapeDtypeStruct((b, D_ATTN, s), BF16),
            jax.ShapeDtypeStruct((b, s, D_ATTN), BF16),
            jax.ShapeDtypeStruct((b, s, D_ATTN), BF16),
            jax.ShapeDtypeStruct((b, D_ATTN, s), BF16),
            jax.ShapeDtypeStruct((b, s, 3 * D_MLSTM), BF16),
            jax.ShapeDtypeStruct((b, s, D_MLSTM), F32),
            jax.ShapeDtypeStruct((b, s, D_SGU), BF16),
            jax.ShapeDtypeStruct((b, s, 3 * LANES), F32),
            jax.ShapeDtypeStruct((b, LANES, s), F32),
        ],
        scratch_shapes=[pltpu.VMEM((16, LANES), F32)],
        compiler_params=pltpu.CompilerParams(
            dimension_semantics=("arbitrary", "arbitrary"), vmem_limit_bytes=VMEM_LIMIT),
        name="in_projection",
    )(x, sh, sc, *consts)


def _attn_kernel(qT_ref, k_ref, kb_ref, vT_ref, o_ref, m_ref, bm_ref, acc_ref, s_ref):
    tq = qT_ref.shape[2]
    tk = tq
    qi = pl.program_id(2)

    qT = qT_ref[0]
    row = lax.broadcasted_iota(jnp.int32, qT.shape, 0)
    rhs = []
    for j in range(2):
        qp = jnp.where((row >= j * HEAD_DIM) & (row < (j + 1) * HEAD_DIM), qT, jnp.zeros_like(qT))
        sel = jnp.where((row >= BIAS_PARTS * j) & (row < BIAS_PARTS * (j + 1)), 1.0, 0.0).astype(BF16)
        rhs.append(jnp.concatenate([qp, sel], axis=0))

    m_ref[...] = jnp.full(m_ref.shape, NEG_BIG, F32)
    acc_ref[...] = jnp.zeros(acc_ref.shape, F32)
    ones = jnp.ones((ONES_ROWS, tk), BF16)

    def scores(kb, j, masked):
        off = pl.multiple_of(kb * tk, tk)
        lhs = jnp.concatenate([k_ref[0, pl.ds(off, tk), :], kb_ref[0, pl.ds(off, tk), :]], axis=1)
        s = _dot(lhs, rhs[j])
        if masked:
            kidx = lax.broadcasted_iota(jnp.int32, s.shape, 0)
            qidx = lax.broadcasted_iota(jnp.int32, s.shape, 1)
            s = jnp.where(kidx <= qidx, s, NEG_BIG)
        s_ref[j] = s
        bm_ref[j, 0:1, :] = jnp.max(s, axis=0, keepdims=True)

    def probs(j):
        m_old = m_ref[j, 0:1, :]
        m_new = jnp.maximum(m_old, bm_ref[j, 0:1, :])
        m_ref[j, 0:1, :] = m_new
        return jnp.exp2(s_ref[j] - m_new).astype(BF16), jnp.exp2(m_old - m_new)

    def accumulate(kb, j, p, alpha):
        off = pl.multiple_of(kb * tk, tk)
        vt = jnp.concatenate(
            [vT_ref[0, j * HEAD_DIM:(j + 1) * HEAD_DIM, pl.ds(off, tk)], ones], axis=0)
        acc_ref[j] = alpha * acc_ref[j] + _dot(vt, p)

    def pipelined(kb, next_masked):
        for j in range(2):
            p, alpha = probs(j)
            scores(kb + 1, j, next_masked)
            accumulate(kb, j, p, alpha)

    @pl.when(qi == 0)
    def _():
        for j in range(2):
            scores(0, j, True)

    @pl.when(qi > 0)
    def _():
        for j in range(2):
            scores(0, j, False)

        def body(kb, carry):
            pipelined(kb, False)
            return carry

        lax.fori_loop(0, qi - 1, body, 0)
        pipelined(qi - 1, True)

    for j in range(2):
        p, alpha = probs(j)
        accumulate(qi, j, p, alpha)
    outT = jnp.concatenate(
        [acc_ref[j, 0:HEAD_DIM, :] / acc_ref[j, HEAD_DIM:HEAD_DIM + 1, :] for j in range(2)], axis=0)
    o_ref[0] = outT.T.astype(BF16)


def _fox_attention(qT, k, kb, vT):
    b, s, _ = k.shape
    tq = ATTN_TILE
    pairs = N_ATTN_HEADS // 2
    return pl.pallas_call(
        _attn_kernel,
        grid=(b, pairs, s // tq),
        in_specs=[
            pl.BlockSpec((1, 2 * HEAD_DIM, tq), lambda bi, p, qi: (bi, p, qi)),
            pl.BlockSpec((1, s, 2 * HEAD_DIM), lambda bi, p, qi: (bi, 0, p)),
            pl.BlockSpec((1, s, LANES), lambda bi, p, qi: (bi, 0, p)),
            pl.BlockSpec((1, 2 * HEAD_DIM, s), lambda bi, p, qi: (bi, p, 0)),
        ],
        out_specs=pl.BlockSpec((1, tq, 2 * HEAD_DIM), lambda bi, p, qi: (bi, qi, p)),
        out_shape=jax.ShapeDtypeStruct((b, s, D_ATTN), BF16),
        scratch_shapes=[pltpu.VMEM((2, 8, tq), F32),
                        pltpu.VMEM((2, 8, tq), F32),
                        pltpu.VMEM((2, HEAD_DIM + ONES_ROWS, tq), F32),
                        pltpu.VMEM((2, tq, tq), F32)],
        compiler_params=pltpu.CompilerParams(
            dimension_semantics=("arbitrary", "arbitrary", "arbitrary"),
            vmem_limit_bytes=VMEM_LIMIT),
        name="fox_attention",
    )(qT, k, kb, vT)


def _mlstm_kernel(qkv_ref, osig_ref, gn_ref, gT_ref, e256_ref, gmo_ref, o_ref, cn_ref, acs_ref):
    L = qkv_ref.shape[1]
    dm = D_MLSTM

    @pl.when(pl.program_id(1) == 0)
    def _():
        cn_ref[...] = jnp.zeros_like(cn_ref)
        acs_ref[...] = jnp.zeros_like(acs_ref)

    q = qkv_ref[0, :, 0:dm]
    k = qkv_ref[0, :, dm:2 * dm]
    v = qkv_ref[0, :, 2 * dm:3 * dm]
    a_nat = gn_ref[0, :, 0:LANES]
    A_nat = gn_ref[0, :, LANES:2 * LANES]
    m_nat = gn_ref[0, :, 2 * LANES:3 * LANES]
    aT = gT_ref[0]
    A_prev = acs_ref[0:1, :]
    A_end = A_nat[L - 1:L, :]

    grp = lax.broadcasted_iota(jnp.int32, (L, dm), 1) // HEAD_DIM
    causal = (lax.broadcasted_iota(jnp.int32, (L, L), 0) >= lax.broadcasted_iota(jnp.int32, (L, L), 1))
    qcn = _dot(q, cn_ref[...].astype(BF16))
    zero_q = jnp.zeros_like(q)
    hout = jnp.zeros((L, dm), F32)
    wst = jnp.zeros((L, dm), F32)
    for hd in range(N_MLSTM_HEADS):
        gl = GATE_LANE_M + hd
        a_row = aT[gl:gl + 1, :]
        A_col = A_nat[:, gl:gl + 1]
        w = jnp.exp(jnp.where(causal, a_row - A_col, -jnp.inf))
        sw = _nt_dot(jnp.where(grp == hd, q, zero_q), k) * w
        den_intra = jnp.sum(sw, axis=1, keepdims=True)
        pv = _dot(sw.astype(BF16), v)
        e_col = jnp.exp(A_prev[:, gl:gl + 1] - A_col)
        den = den_intra + e_col * qcn[:, dm + hd:dm + hd + 1]
        r = 1.0 / jnp.maximum(jnp.abs(den), jnp.exp(-m_nat[:, gl:gl + 1]))
        hout = jnp.where(grp == hd, (pv + e_col * qcn[:, 0:dm]) * r, hout)
        wst = jnp.where(grp == hd, jnp.exp(a_nat[:, gl:gl + 1] - A_end[:, gl:gl + 1]), wst)

    msq = _split_dot(hout * hout, e256_ref[...])
    hn = hout * lax.rsqrt(msq + EPS) * gmo_ref[...]
    o_ref[0] = (osig_ref[0] * hn).astype(BF16)

    kw = (k.astype(F32) * wst).astype(BF16)
    rhs = jnp.concatenate([v, jnp.ones((L, LANES), BF16)], axis=1)
    upd = _tn_dot(kw, rhs)
    rowg = lax.broadcasted_iota(jnp.int32, upd.shape, 0) // HEAD_DIM
    col = lax.broadcasted_iota(jnp.int32, upd.shape, 1)
    keep = jnp.where(col < dm, col // HEAD_DIM, col - dm) == rowg
    dvec = jnp.exp(A_prev - A_end)
    decay = jnp.zeros(upd.shape, F32)
    for hd in range(N_MLSTM_HEADS):
        gl = GATE_LANE_M + hd
        decay = jnp.where(rowg == hd, dvec[:, gl:gl + 1], decay)
    cn_ref[...] = decay * cn_ref[...] + jnp.where(keep, upd, 0.0)
    acs_ref[0:1, :] = A_end


def _mlstm(qkvm, osig, gn, gT, e256, gmo):
    b, s, _ = qkvm.shape
    L = MLSTM_CHUNK
    return pl.pallas_call(
        _mlstm_kernel,
        grid=(b, s // L),
        in_specs=[
            pl.BlockSpec((1, L, 3 * D_MLSTM), lambda bi, ci: (bi, ci, 0)),
            pl.BlockSpec((1, L, D_MLSTM), lambda bi, ci: (bi, ci, 0)),
            pl.BlockSpec((1, L, 3 * LANES), lambda bi, ci: (bi, ci, 0)),
            pl.BlockSpec((1, LANES, L), lambda bi, ci: (bi, 0, ci)),
            _const_spec(e256.shape),
            _const_spec(gmo.shape),
        ],
        out_specs=pl.BlockSpec((1, L, D_MLSTM), lambda bi, ci: (bi, ci, 0)),
        out_shape=jax.ShapeDtypeStruct((b, s, D_MLSTM), BF16),
        scratch_shapes=[pltpu.VMEM((D_MLSTM, D_MLSTM + LANES), F32), pltpu.VMEM((8, LANES), F32)],
        compiler_params=pltpu.CompilerParams(
            dimension_semantics=("arbitrary", "arbitrary"), vmem_limit_bytes=VMEM_LIMIT),
        name="mlstm",
    )(qkvm, osig, gn, gT, e256, gmo)


def _post_kernel(x_ref, ya_ref, ym_ref, ys_ref, gt1_ref, sh2_ref, sc2_ref, gt2_ref, g_ref,
                 wo_ref, wup_ref, wconv_ref, bconv_ref, wdown_ref, o_ref, halo_ref, hact_ref):
    tm = x_ref.shape[1]

    @pl.when(pl.program_id(1) == 0)
    def _():
        halo_ref[...] = jnp.zeros_like(halo_ref)

    y = (_dot(ya_ref[0], wo_ref[0:D_ATTN, :])
         + _dot(ym_ref[0], wo_ref[D_ATTN:D_ATTN + D_MLSTM, :])
         + _dot(ys_ref[0], wo_ref[D_ATTN + D_MLSTM:D_MODEL, :]))
    x1 = x_ref[0] + gt1_ref[0] * y
    ms = jnp.mean(x1 * x1, axis=-1, keepdims=True)
    h2 = (x1 * lax.rsqrt(ms + EPS) * g_ref[...] * (1.0 + sc2_ref[0]) + sh2_ref[0]).astype(BF16)

    row = lax.broadcasted_iota(jnp.int32, (tm, FF_CHUNK), 0)

    def conv(cols):
        z = _dot(h2, wup_ref[:, cols])
        prev2 = halo_ref[6:7, cols]
        prev1 = halo_ref[7:8, cols]
        halo_ref[:, cols] = z[tm - 8:tm, :]
        z1 = jnp.where(row == 0, prev1, pltpu.roll(z, 1, axis=0))
        z2 = jnp.where(row == 0, prev2, jnp.where(row == 1, prev1, pltpu.roll(z, 2, axis=0)))
        return (wconv_ref[0:1, cols] * z2 + wconv_ref[1:2, cols] * z1 + wconv_ref[2:3, cols] * z
                + bconv_ref[:, cols])

    for c in range(D_FF // FF_CHUNK):
        a = conv(pl.ds(c * FF_CHUNK, FF_CHUNK))
        gv = conv(pl.ds(D_FF + c * FF_CHUNK, FF_CHUNK))
        hact_ref[:, c * FF_CHUNK:(c + 1) * FF_CHUNK] = (a * jax.nn.sigmoid(a) * gv).astype(BF16)

    o_ref[0] = x1 + gt2_ref[0] * _dot(hact_ref[...], wdown_ref[...])


def _post(x, ya, ym, ys, gt1, sh2, sc2, gt2, g, wo, wup, wconv, bconv, wdown):
    b, s, d = x.shape
    tm = POST_TILE
    tok = lambda width: pl.BlockSpec((1, tm, width), lambda bi, si: (bi, si, 0))
    mod = pl.BlockSpec((1, 1, d), lambda bi, si: (bi, 0, 0))
    consts = [g, wo, wup, wconv, bconv, wdown]
    return pl.pallas_call(
        _post_kernel,
        grid=(b, s // tm),
        in_specs=[tok(d), tok(D_ATTN), tok(D_MLSTM), tok(D_SGU), mod, mod, mod, mod]
        + [_const_spec(c.shape) for c in consts],
        out_specs=tok(d),
        out_shape=jax.ShapeDtypeStruct((b, s, d), F32),
        scratch_shapes=[pltpu.VMEM((8, 2 * D_FF), F32), pltpu.VMEM((tm, D_FF), BF16)],
        compiler_params=pltpu.CompilerParams(
            dimension_semantics=("arbitrary", "arbitrary"), vmem_limit_bytes=VMEM_LIMIT),
        name="out_proj_ffn",
    )(x, ya, ym, ys, gt1, sh2, sc2, gt2, *consts)


def _block_diag_mean(n):
    idx = jnp.arange(n) // HEAD_DIM
    return jnp.where(idx[:, None] == idx[None, :], 1.0 / HEAD_DIM, 0.0).astype(BF16)


def _prep_layer(w_in, b_fox_f, b_m_i, b_m_f, g_q, g_k, g_m_out, g_sgu, w_s, b_s):
    o = 0
    wq = w_in[:, o:o + D_ATTN]; o += D_ATTN
    wk = w_in[:, o:o + D_ATTN]; o += D_ATTN
    wv = w_in[:, o:o + D_ATTN]; o += D_ATTN
    wm = w_in[:, o:o + 4 * D_MLSTM]; o += 4 * D_MLSTM
    ws = w_in[:, o:o + 2 * D_SGU]; o += 2 * D_SGU
    wfa = w_in[:, o:o + N_ATTN_HEADS]; o += N_ATTN_HEADS
    wim = w_in[:, o:o + N_MLSTM_HEADS]; o += N_MLSTM_HEADS
    wfm = w_in[:, o:o + N_MLSTM_HEADS]
    lo, hi = GATE_LANE_M, GATE_LANE_M + N_MLSTM_HEADS
    wg = jnp.zeros((D_MODEL, 2 * LANES), F32)
    wg = wg.at[:, 0:N_ATTN_HEADS].set(wfa).at[:, lo:hi].set(wfm).at[:, LANES + lo:LANES + hi].set(wim)
    gbias = jnp.zeros((1, 2 * LANES), F32)
    gbias = (gbias.at[0, 0:N_ATTN_HEADS].set(b_fox_f).at[0, lo:hi].set(b_m_f)
             .at[0, LANES + lo:LANES + hi].set(b_m_i))
    heads = jnp.arange(N_ATTN_HEADS)
    place = jnp.zeros((BIAS_PARTS * LANES, D_ATTN), F32)
    for r in range(BIAS_PARTS):
        place = place.at[r * LANES + heads, (heads // 2) * LANES + BIAS_PARTS * (heads % 2) + r].set(1.0)
    return dict(
        place=place.astype(BF16),
        wqT=wq.T.astype(BF16), wk=wk.astype(BF16), wvT=wv.T.astype(BF16), wm=wm.astype(BF16),
        ws=ws.astype(BF16), wg=wg.astype(BF16), gbias=gbias,
        gq=g_q.reshape(HEAD_DIM, 1), gk=jnp.tile(g_k, N_ATTN_HEADS).reshape(1, D_ATTN),
        e512=_block_diag_mean(D_ATTN), e256=_block_diag_mean(D_SGU),
        gsgu=g_sgu.reshape(1, D_SGU), wst=w_s.reshape(N_SGU_GROUPS * SGU_CHUNK, SGU_CHUNK),
        bst=jnp.repeat(b_s.T, HEAD_DIM, axis=1), gmo=g_m_out.reshape(1, D_MLSTM),
    )


def kernel(x, c, w_ada, b_ada, g_mix, g_ffn, w_in, b_fox_f, b_m_i, b_m_f, g_q, g_k, g_m_out, g_sgu, w_s, b_s, w_out, w_up, w_conv, b_conv, w_down):
    depth = w_ada.shape[0]
    b, s, d = x.shape
    assert d == D_MODEL and s % max(IN_TILE, POST_TILE, MLSTM_CHUNK, ATTN_TILE) == 0
    mod = _modulation(c, w_ada, b_ada)
    for l in range(depth):
        sh1, sc1, gt1, sh2, sc2, gt2 = [mod[l, :, i * d:(i + 1) * d].reshape(b, 1, d) for i in range(6)]
        w = _prep_layer(w_in[l], b_fox_f[l], b_m_i[l], b_m_f[l], g_q[l], g_k[l], g_m_out[l],
                        g_sgu[l], w_s[l], b_s[l])
        qT, k, kb, vT, qkvm, osig, ysgu, gn, gT = _in_projection(x, sh1, sc1, g_mix[l].reshape(1, d), w)
        ya = _fox_attention(qT, k, kb, vT)
        ym = _mlstm(qkvm, osig, gn, gT, w["e256"], w["gmo"])
        x = _post(x, ya, ym, ysgu, gt1, sh2, sc2, gt2, g_ffn[l].reshape(1, d),
                  w_out[l].astype(BF16), w_up[l].astype(BF16), w_conv[l],
                  b_conv[l].reshape(1, 2 * D_FF), w_down[l].astype(BF16))
    return x
```

```python
import functools

import jax
import jax.numpy as jnp
from jax import lax
from jax.experimental import pallas as pl
from jax.experimental.pallas import tpu as pltpu

F32 = jnp.float32
BF16 = jnp.bfloat16

D_MODEL = 1024
HEAD_DIM = 64
D_ATTN = 512
D_MLSTM = 256
D_SGU = 256
N_ATTN_HEADS = 8
N_MLSTM_HEADS = 4
N_SGU_GROUPS = 4
SGU_CHUNK = 128
D_FF = 2816
EPS = 1e-6
LOG2E = 1.4426950408889634

LANES = 128
IN_TILE = 512
ATTN_TILE = 512
BIAS_PARTS = 3
ONES_ROWS = 16
MLSTM_CHUNK = 256
POST_TILE = 512
FF_CHUNK = 256
GATE_LANE_M = 8
NEG_BIG = -1e30
VMEM_LIMIT = 56 * 1024 * 1024


def _nt_dot(a, b):
    return lax.dot_general(a, b, (((1,), (1,)), ((), ())), preferred_element_type=F32)


def _tn_dot(a, b):
    return lax.dot_general(a, b, (((0,), (0,)), ((), ())), preferred_element_type=F32)


def _dot(a, b):
    return jnp.dot(a, b, preferred_element_type=F32)


def _split_dot(x, e):
    hi = x.astype(BF16)
    lo = (x - hi.astype(F32)).astype(BF16)
    return _dot(hi, e) + _dot(lo, e)


def _scan_rows(x, op, identity):
    n = x.shape[0]
    row = lax.broadcasted_iota(jnp.int32, x.shape, 0)
    k = 1
    while k < n:
        shifted = pltpu.roll(x, k, axis=0)
        x = op(x, jnp.where(row >= k, shifted, identity))
        k *= 2
    return x


def _mod_kernel(c_ref, w_ref, b_ref, o_ref):
    c = c_ref[...]
    ca = c * jax.nn.sigmoid(c)
    o_ref[0] = jnp.dot(ca, w_ref[0], preferred_element_type=F32,
                       precision=lax.Precision.HIGHEST) + b_ref[0]


def _modulation(c, w_ada, b_ada):
    depth, d, n = w_ada.shape
    b = c.shape[0]
    tn = 1536
    return pl.pallas_call(
        _mod_kernel,
        grid=(depth, n // tn),
        in_specs=[
            pl.BlockSpec((b, d), lambda l, j: (0, 0)),
            pl.BlockSpec((1, d, tn), lambda l, j: (l, 0, j)),
            pl.BlockSpec((1, 1, tn), lambda l, j: (l, 0, j)),
        ],
        out_specs=pl.BlockSpec((1, b, tn), lambda l, j: (l, 0, j)),
        out_shape=jax.ShapeDtypeStruct((depth, b, n), F32),
        compiler_params=pltpu.CompilerParams(
            dimension_semantics=("arbitrary", "arbitrary"), vmem_limit_bytes=VMEM_LIMIT),
        name="modulation",
    )(c, w_ada, b_ada.reshape(depth, 1, n))


def _in_kernel(x_ref, sh_ref, sc_ref, g_ref, wqT_ref, wk_ref, wvT_ref, wm_ref, ws_ref, wg_ref,
               gq_ref, gk_ref, e512_ref, e256_ref, gsgu_ref, wst_ref, bst_ref, gbias_ref, place_ref,
               qT_ref, k_ref, kb_ref, vT_ref, qkvm_ref, osig_ref, ysgu_ref, gn_ref, gT_ref,
               carry_ref):
    tm = x_ref.shape[1]

    @pl.when(pl.program_id(1) == 0)
    def _():
        carry_ref[...] = jnp.zeros_like(carry_ref)

    x = x_ref[0]
    ms = jnp.mean(x * x, axis=-1, keepdims=True)
    h = x * lax.rsqrt(ms + EPS) * g_ref[...] * (1.0 + sc_ref[0]) + sh_ref[0]
    hb = h.astype(BF16)

    qT = _nt_dot(wqT_ref[...], hb)
    q_scale = (HEAD_DIM ** -0.5) * LOG2E
    for hd in range(N_ATTN_HEADS):
        blk = qT[hd * HEAD_DIM:(hd + 1) * HEAD_DIM]
        msq = jnp.mean(blk * blk, axis=0, keepdims=True)
        qn = blk * lax.rsqrt(msq + EPS) * (gq_ref[...] * q_scale)
        qT_ref[0, hd * HEAD_DIM:(hd + 1) * HEAD_DIM, :] = qn.astype(BF16)

    kk = _dot(hb, wk_ref[...])
    msk = _dot((kk * kk).astype(BF16), e512_ref[...])
    k_ref[0] = (kk * lax.rsqrt(msk + EPS) * gk_ref[...]).astype(BF16)

    vT_ref[0] = _nt_dot(wvT_ref[...], hb).astype(BF16)

    pm = _dot(hb, wm_ref[...])
    qkvm_ref[0, :, 0:D_MLSTM] = pm[:, 0:D_MLSTM].astype(BF16)
    qkvm_ref[0, :, D_MLSTM:2 * D_MLSTM] = (pm[:, D_MLSTM:2 * D_MLSTM] * HEAD_DIM ** -0.5).astype(BF16)
    qkvm_ref[0, :, 2 * D_MLSTM:3 * D_MLSTM] = pm[:, 2 * D_MLSTM:3 * D_MLSTM].astype(BF16)
    osig_ref[0] = jax.nn.sigmoid(pm[:, 3 * D_MLSTM:4 * D_MLSTM])

    pg = _dot(hb, wg_ref[...]) + gbias_ref[...]
    lane = lax.broadcasted_iota(jnp.int32, (tm, LANES), 1)
    ls = jnp.where(lane < GATE_LANE_M + N_MLSTM_HEADS, jax.nn.log_sigmoid(pg[:, 0:LANES]), 0.0)
    cs = _scan_rows(ls, jnp.add, 0.0) + carry_ref[0:1, :]
    a = pg[:, LANES:2 * LANES] - cs
    cm = jnp.maximum(_scan_rows(a, jnp.maximum, -jnp.inf), carry_ref[8:9, :])
    carry_ref[0:1, :] = cs[tm - 1:tm, :]
    carry_ref[8:9, :] = cm[tm - 1:tm, :]
    nf = -LOG2E * cs
    hi = nf.astype(BF16)
    r1 = nf - hi.astype(F32)
    mid = r1.astype(BF16)
    lo = (r1 - mid.astype(F32)).astype(BF16)
    kb_ref[0] = _dot(jnp.concatenate([hi, mid, lo], axis=1), place_ref[...]).astype(BF16)
    g1 = jnp.where(lane < GATE_LANE_M, nf, a)
    gn_ref[0, :, 0:LANES] = g1
    gn_ref[0, :, LANES:2 * LANES] = cm
    gn_ref[0, :, 2 * LANES:3 * LANES] = cs + cm
    gT_ref[0] = g1.T

    ps = _dot(hb, ws_ref[...])
    gu = jax.nn.gelu(ps[:, 0:D_SGU])
    gv = jax.nn.gelu(ps[:, D_SGU:2 * D_SGU])
    e256 = e256_ref[...]
    xc = gv - _split_dot(gv, e256)
    var = _split_dot(xc * xc, e256)
    vnb = (xc * lax.rsqrt(var + EPS) * gsgu_ref[...]).astype(BF16)
    nrow = N_SGU_GROUPS * SGU_CHUNK
    wrow = lax.broadcasted_iota(jnp.int32, (nrow, SGU_CHUNK), 0)
    wcol = lax.broadcasted_iota(jnp.int32, (nrow, SGU_CHUNK), 1)
    wst = jnp.where((wrow % SGU_CHUNK) >= wcol, wst_ref[...], 0.0).astype(BF16)
    grp = lax.broadcasted_iota(jnp.int32, (SGU_CHUNK, D_SGU), 1) // HEAD_DIM
    for c in range(tm // SGU_CHUNK):
        r = _dot(wst, vnb[c * SGU_CHUNK:(c + 1) * SGU_CHUNK])
        mixed = r[0:SGU_CHUNK]
        for gidx in range(1, N_SGU_GROUPS):
            mixed = jnp.where(grp == gidx, r[gidx * SGU_CHUNK:(gidx + 1) * SGU_CHUNK], mixed)
        ysgu_ref[0, c * SGU_CHUNK:(c + 1) * SGU_CHUNK, :] = (
            gu[c * SGU_CHUNK:(c + 1) * SGU_CHUNK] * (mixed + bst_ref[...])).astype(BF16)


def _const_spec(shape):
    nd = len(shape)
    return pl.BlockSpec(shape, lambda b, s: (0,) * nd, pipeline_mode=pl.Buffered(1))


def _in_projection(x, sh, sc, g, w):
    b, s, d = x.shape
    tm = IN_TILE
    tok = lambda width: pl.BlockSpec((1, tm, width), lambda bi, si: (bi, si, 0))
    tokT = lambda rows: pl.BlockSpec((1, rows, tm), lambda bi, si: (bi, 0, si))
    mod = pl.BlockSpec((1, 1, d), lambda bi, si: (bi, 0, 0))
    consts = [g, w["wqT"], w["wk"], w["wvT"], w["wm"], w["ws"], w["wg"], w["gq"], w["gk"],
              w["e512"], w["e256"], w["gsgu"], w["wst"], w["bst"], w["gbias"], w["place"]]
    return pl.pallas_call(
        _in_kernel,
        grid=(b, s // tm),
        in_specs=[tok(d), mod, mod] + [_const_spec(c.shape) for c in consts],
        out_specs=[tokT(D_ATTN), tok(D_ATTN), tok(D_ATTN), tokT(D_ATTN), tok(3 * D_MLSTM),
                   tok(D_MLSTM), tok(D_SGU), tok(3 * LANES), tokT(LANES)],
        out_shape=[
            jax.ShapeDtypeStruct((b, D_ATTN, s), BF16),
            jax.ShapeDtypeStruct((b, s, D_ATTN), BF16),
            jax.ShapeDtypeStruct((b, s, D_ATTN), BF16),
            jax.ShapeDtypeStruct((b, D_ATTN, s), BF16),
            jax.ShapeDtypeStruct((b, s, 3 * D_MLSTM), BF16),
            jax.ShapeDtypeStruct((b, s, D_MLSTM), F32),
            jax.ShapeDtypeStruct((b, s, D_SGU), BF16),
            jax.ShapeDtypeStruct((b, s, 3 * LANES), F32),
            jax.ShapeDtypeStruct((b, LANES, s), F32),
        ],
        scratch_shapes=[pltpu.VMEM((16, LANES), F32)],
        compiler_params=pltpu.CompilerParams(
            dimension_semantics=("arbitrary", "arbitrary"), vmem_limit_bytes=VMEM_LIMIT),
        name="in_projection",
    )(x, sh, sc, *consts)


def _attn_kernel(qT_ref, k_ref, kb_ref, vT_ref, o_ref, m_ref, bm_ref, acc_ref, s_ref):
    tq = qT_ref.shape[2]
    tk = tq
    qi = pl.program_id(2)

    qT = qT_ref[0]
    row = lax.broadcasted_iota(jnp.int32, qT.shape, 0)
    rhs = []
    for j in range(2):
        qp = jnp.where((row >= j * HEAD_DIM) & (row < (j + 1) * HEAD_DIM), qT, jnp.zeros_like(qT))
        sel = jnp.where((row >= BIAS_PARTS * j) & (row < BIAS_PARTS * (j + 1)), 1.0, 0.0).astype(BF16)
        rhs.append(jnp.concatenate([qp, sel], axis=0))

    m_ref[...] = jnp.full(m_ref.shape, NEG_BIG, F32)
    acc_ref[...] = jnp.zeros(acc_ref.shape, F32)
    ones = jnp.ones((ONES_ROWS, tk), BF16)

    def scores(kb, j, masked):
        off = pl.multiple_of(kb * tk, tk)
        lhs = jnp.concatenate([k_ref[0, pl.ds(off, tk), :], kb_ref[0, pl.ds(off, tk), :]], axis=1)
        s = _dot(lhs, rhs[j])
        if masked:
            kidx = lax.broadcasted_iota(jnp.int32, s.shape, 0)
            qidx = lax.broadcasted_iota(jnp.int32, s.shape, 1)
            s = jnp.where(kidx <= qidx, s, NEG_BIG)
        s_ref[j] = s
        bm_ref[j, 0:1, :] = jnp.max(s, axis=0, keepdims=True)

    def probs(j):
        m_old = m_ref[j, 0:1, :]
        m_new = jnp.maximum(m_old, bm_ref[j, 0:1, :])
        m_ref[j, 0:1, :] = m_new
        return jnp.exp2(s_ref[j] - m_new).astype(BF16), jnp.exp2(m_old - m_new)

    def accumulate(kb, j, p, alpha):
        off = pl.multiple_of(kb * tk, tk)
        vt = jnp.concatenate(
            [vT_ref[0, j * HEAD_DIM:(j + 1) * HEAD_DIM, pl.ds(off, tk)], ones], axis=0)
        acc_ref[j] = alpha * acc_ref[j] + _dot(vt, p)

    def pipelined(kb, next_masked):
        for j in range(2):
            p, alpha = probs(j)
            scores(kb + 1, j, next_masked)
            accumulate(kb, j, p, alpha)

    def finish():
        for j in range(2):
            p, alpha = probs(j)
            accumulate(qi, j, p, alpha)
        outT = jnp.concatenate(
            [acc_ref[j, 0:HEAD_DIM, :] / acc_ref[j, HEAD_DIM:HEAD_DIM + 1, :] for j in range(2)],
            axis=0)
        o_ref[0] = outT.T.astype(BF16)

    @pl.when(qi == 0)
    def _():
        for j in range(2):
            scores(0, j, True)
        finish()

    @pl.when(qi > 0)
    def _():
        for j in range(2):
            scores(0, j, False)

        n_plain = qi - 1

        def body(i, carry):
            pipelined(2 * i, False)
            pipelined(2 * i + 1, False)
            return carry

        lax.fori_loop(0, lax.shift_right_logical(n_plain, 1), body, 0)

        @pl.when((n_plain & 1) == 1)
        def _():
            pipelined(n_plain - 1, False)

        pipelined(qi - 1, True)
        finish()


def _fox_attention(qT, k, kb, vT):
    b, s, _ = k.shape
    tq = ATTN_TILE
    pairs = N_ATTN_HEADS // 2
    return pl.pallas_call(
        _attn_kernel,
        grid=(b, pairs, s // tq),
        in_specs=[
            pl.BlockSpec((1, 2 * HEAD_DIM, tq), lambda bi, p, qi: (bi, p, qi)),
            pl.BlockSpec((1, s, 2 * HEAD_DIM), lambda bi, p, qi: (bi, 0, p)),
            pl.BlockSpec((1, s, LANES), lambda bi, p, qi: (bi, 0, p)),
            pl.BlockSpec((1, 2 * HEAD_DIM, s), lambda bi, p, qi: (bi, p, 0)),
        ],
        out_specs=pl.BlockSpec((1, tq, 2 * HEAD_DIM), lambda bi, p, qi: (bi, qi, p)),
        out_shape=jax.ShapeDtypeStruct((b, s, D_ATTN), BF16),
        scratch_shapes=[pltpu.VMEM((2, 8, tq), F32),
                        pltpu.VMEM((2, 8, tq), F32),
                        pltpu.VMEM((2, HEAD_DIM + ONES_ROWS, tq), F32),
                        pltpu.VMEM((2, tq, tq), F32)],
        compiler_params=pltpu.CompilerParams(
            dimension_semantics=("arbitrary", "arbitrary", "arbitrary"),
            vmem_limit_bytes=VMEM_LIMIT),
        name="fox_attention",
    )(qT, k, kb, vT)


def _mlstm_kernel(qkv_ref, osig_ref, gn_ref, gT_ref, e256_ref, gmo_ref, o_ref, cn_ref, acs_ref):
    L = qkv_ref.shape[1]
    dm = D_MLSTM

    @pl.when(pl.program_id(1) == 0)
    def _():
        cn_ref[...] = jnp.zeros_like(cn_ref)
        acs_ref[...] = jnp.zeros_like(acs_ref)

    q = qkv_ref[0, :, 0:dm]
    k = qkv_ref[0, :, dm:2 * dm]
    v = qkv_ref[0, :, 2 * dm:3 * dm]
    a_nat = gn_ref[0, :, 0:LANES]
    A_nat = gn_ref[0, :, LANES:2 * LANES]
    m_nat = gn_ref[0, :, 2 * LANES:3 * LANES]
    aT = gT_ref[0]
    A_prev = acs_ref[0:1, :]
    A_end = A_nat[L - 1:L, :]

    grp = lax.broadcasted_iota(jnp.int32, (L, dm), 1) // HEAD_DIM
    causal = (lax.broadcasted_iota(jnp.int32, (L, L), 0) >= lax.broadcasted_iota(jnp.int32, (L, L), 1))
    qcn = _dot(q, cn_ref[...].astype(BF16))
    zero_q = jnp.zeros_like(q)
    hout = jnp.zeros((L, dm), F32)
    wst = jnp.zeros((L, dm), F32)
    for hd in range(N_MLSTM_HEADS):
        gl = GATE_LANE_M + hd
        a_row = aT[gl:gl + 1, :]
        A_col = A_nat[:, gl:gl + 1]
        w = jnp.exp(jnp.where(causal, a_row - A_col, -jnp.inf))
        sw = _nt_dot(jnp.where(grp == hd, q, zero_q), k) * w
        den_intra = jnp.sum(sw, axis=1, keepdims=True)
        pv = _dot(sw.astype(BF16), v)
        e_col = jnp.exp(A_prev[:, gl:gl + 1] - A_col)
        den = den_intra + e_col * qcn[:, dm + hd:dm + hd + 1]
        r = 1.0 / jnp.maximum(jnp.abs(den), jnp.exp(-m_nat[:, gl:gl + 1]))
        hout = jnp.where(grp == hd, (pv + e_col * qcn[:, 0:dm]) * r, hout)
        wst = jnp.where(grp == hd, jnp.exp(a_nat[:, gl:gl + 1] - A_end[:, gl:gl + 1]), wst)

    msq = _split_dot(hout * hout, e256_ref[...])
    hn = hout * lax.rsqrt(msq + EPS) * gmo_ref[...]
    o_ref[0] = (osig_ref[0] * hn).astype(BF16)

    kw = (k.astype(F32) * wst).astype(BF16)
    rhs = jnp.concatenate([v, jnp.ones((L, LANES), BF16)], axis=1)
    upd = _tn_dot(kw, rhs)
    rowg = lax.broadcasted_iota(jnp.int32, upd.shape, 0) // HEAD_DIM
    col = lax.broadcasted_iota(jnp.int32, upd.shape, 1)
    keep = jnp.where(col < dm, col // HEAD_DIM, col - dm) == rowg
    dvec = jnp.exp(A_prev - A_end)
    decay = jnp.zeros(upd.shape, F32)
    for hd in range(N_MLSTM_HEADS):
        gl = GATE_LANE_M + hd
        decay = jnp.where(rowg == hd, dvec[:, gl:gl + 1], decay)
    cn_ref[...] = decay * cn_ref[...] + jnp.where(keep, upd, 0.0)
    acs_ref[0:1, :] = A_end


def _mlstm(qkvm, osig, gn, gT, e256, gmo):
    b, s, _ = qkvm.shape
    L = MLSTM_CHUNK
    return pl.pallas_call(
        _mlstm_kernel,
        grid=(b, s // L),
        in_specs=[
            pl.BlockSpec((1, L, 3 * D_MLSTM), lambda bi, ci: (bi, ci, 0)),
            pl.BlockSpec((1, L, D_MLSTM), lambda bi, ci: (bi, ci, 0)),
            pl.BlockSpec((1, L, 3 * LANES), lambda bi, ci: (bi, ci, 0)),
            pl.BlockSpec((1, LANES, L), lambda bi, ci: (bi, 0, ci)),
            _const_spec(e256.shape),
            _const_spec(gmo.shape),
        ],
        out_specs=pl.BlockSpec((1, L, D_MLSTM), lambda bi, ci: (bi, ci, 0)),
        out_shape=jax.ShapeDtypeStruct((b, s, D_MLSTM), BF16),
        scratch_shapes=[pltpu.VMEM((D_MLSTM, D_MLSTM + LANES), F32), pltpu.VMEM((8, LANES), F32)],
        compiler_params=pltpu.CompilerParams(
            dimension_semantics=("arbitrary", "arbitrary"), vmem_limit_bytes=VMEM_LIMIT),
        name="mlstm",
    )(qkvm, osig, gn, gT, e256, gmo)


def _post_kernel(x_ref, ya_ref, ym_ref, ys_ref, gt1_ref, sh2_ref, sc2_ref, gt2_ref, g_ref,
                 wo_ref, wup_ref, wconv_ref, bconv_ref, wdown_ref, o_ref, halo_ref, hact_ref):
    tm = x_ref.shape[1]

    @pl.when(pl.program_id(1) == 0)
    def _():
        halo_ref[...] = jnp.zeros_like(halo_ref)

    y = (_dot(ya_ref[0], wo_ref[0:D_ATTN, :])
         + _dot(ym_ref[0], wo_ref[D_ATTN:D_ATTN + D_MLSTM, :])
         + _dot(ys_ref[0], wo_ref[D_ATTN + D_MLSTM:D_MODEL, :]))
    x1 = x_ref[0] + gt1_ref[0] * y
    ms = jnp.mean(x1 * x1, axis=-1, keepdims=True)
    h2 = (x1 * lax.rsqrt(ms + EPS) * g_ref[...] * (1.0 + sc2_ref[0]) + sh2_ref[0]).astype(BF16)

    row = lax.broadcasted_iota(jnp.int32, (tm, FF_CHUNK), 0)

    def conv(cols):
        z = _dot(h2, wup_ref[:, cols])
        prev2 = halo_ref[6:7, cols]
        prev1 = halo_ref[7:8, cols]
        halo_ref[:, cols] = z[tm - 8:tm, :]
        z1 = jnp.where(row == 0, prev1, pltpu.roll(z, 1, axis=0))
        z2 = jnp.where(row == 0, prev2, jnp.where(row == 1, prev1, pltpu.roll(z, 2, axis=0)))
        return (wconv_ref[0:1, cols] * z2 + wconv_ref[1:2, cols] * z1 + wconv_ref[2:3, cols] * z
                + bconv_ref[:, cols])

    for c in range(D_FF // FF_CHUNK):
        a = conv(pl.ds(c * FF_CHUNK, FF_CHUNK))
        gv = conv(pl.ds(D_FF + c * FF_CHUNK, FF_CHUNK))
        hact_ref[:, c * FF_CHUNK:(c + 1) * FF_CHUNK] = (a * jax.nn.sigmoid(a) * gv).astype(BF16)

    o_ref[0] = x1 + gt2_ref[0] * _dot(hact_ref[...], wdown_ref[...])


def _post(x, ya, ym, ys, gt1, sh2, sc2, gt2, g, wo, wup, wconv, bconv, wdown):
    b, s, d = x.shape
    tm = POST_TILE
    tok = lambda width: pl.BlockSpec((1, tm, width), lambda bi, si: (bi, si, 0))
    mod = pl.BlockSpec((1, 1, d), lambda bi, si: (bi, 0, 0))
    consts = [g, wo, wup, wconv, bconv, wdown]
    return pl.pallas_call(
        _post_kernel,
        grid=(b, s // tm),
        in_specs=[tok(d), tok(D_ATTN), tok(D_MLSTM), tok(D_SGU), mod, mod, mod, mod]
        + [_const_spec(c.shape) for c in consts],
        out_specs=tok(d),
        out_shape=jax.ShapeDtypeStruct((b, s, d), F32),
        scratch_shapes=[pltpu.VMEM((8, 2 * D_FF), F32), pltpu.VMEM((tm, D_FF), BF16)],
        compiler_params=pltpu.CompilerParams(
            dimension_semantics=("arbitrary", "arbitrary"), vmem_limit_bytes=VMEM_LIMIT),
        name="out_proj_ffn",
    )(x, ya, ym, ys, gt1, sh2, sc2, gt2, *consts)


def _block_diag_mean(n):
    idx = jnp.arange(n) // HEAD_DIM
    return jnp.where(idx[:, None] == idx[None, :], 1.0 / HEAD_DIM, 0.0).astype(BF16)


def _prep_layer(w_in, b_fox_f, b_m_i, b_m_f, g_q, g_k, g_m_out, g_sgu, w_s, b_s):
    o = 0
    wq = w_in[:, o:o + D_ATTN]; o += D_ATTN
    wk = w_in[:, o:o + D_ATTN]; o += D_ATTN
    wv = w_in[:, o:o + D_ATTN]; o += D_ATTN
    wm = w_in[:, o:o + 4 * D_MLSTM]; o += 4 * D_MLSTM
    ws = w_in[:, o:o + 2 * D_SGU]; o += 2 * D_SGU
    wfa = w_in[:, o:o + N_ATTN_HEADS]; o += N_ATTN_HEADS
    wim = w_in[:, o:o + N_MLSTM_HEADS]; o += N_MLSTM_HEADS
    wfm = w_in[:, o:o + N_MLSTM_HEADS]
    lo, hi = GATE_LANE_M, GATE_LANE_M + N_MLSTM_HEADS
    wg = jnp.zeros((D_MODEL, 2 * LANES), F32)
    wg = wg.at[:, 0:N_ATTN_HEADS].set(wfa).at[:, lo:hi].set(wfm).at[:, LANES + lo:LANES + hi].set(wim)
    gbias = jnp.zeros((1, 2 * LANES), F32)
    gbias = (gbias.at[0, 0:N_ATTN_HEADS].set(b_fox_f).at[0, lo:hi].set(b_m_f)
             .at[0, LANES + lo:LANES + hi].set(b_m_i))
    heads = jnp.arange(N_ATTN_HEADS)
    place = jnp.zeros((BIAS_PARTS * LANES, D_ATTN), F32)
    for r in range(BIAS_PARTS):
        place = place.at[r * LANES + heads, (heads // 2) * LANES + BIAS_PARTS * (heads % 2) + r].set(1.0)
    return dict(
        place=place.astype(BF16),
        wqT=wq.T.astype(BF16), wk=wk.astype(BF16), wvT=wv.T.astype(BF16), wm=wm.astype(BF16),
        ws=ws.astype(BF16), wg=wg.astype(BF16), gbias=gbias,
        gq=g_q.reshape(HEAD_DIM, 1), gk=jnp.tile(g_k, N_ATTN_HEADS).reshape(1, D_ATTN),
        e512=_block_diag_mean(D_ATTN), e256=_block_diag_mean(D_SGU),
        gsgu=g_sgu.reshape(1, D_SGU), wst=w_s.reshape(N_SGU_GROUPS * SGU_CHUNK, SGU_CHUNK),
        bst=jnp.repeat(b_s.T, HEAD_DIM, axis=1), gmo=g_m_out.reshape(1, D_MLSTM),
    )


def kernel(x, c, w_ada, b_ada, g_mix, g_ffn, w_in, b_fox_f, b_m_i, b_m_f, g_q, g_k, g_m_out, g_sgu, w_s, b_s, w_out, w_up, w_conv, b_conv, w_down):
    depth = w_ada.shape[0]
    b, s, d = x.shape
    assert d == D_MODEL and s % max(IN_TILE, POST_TILE, MLSTM_CHUNK, ATTN_TILE) == 0
    mod = _modulation(c, w_ada, b_ada)
    for l in range(depth):
        sh1, sc1, gt1, sh2, sc2, gt2 = [mod[l, :, i * d:(i + 1) * d].reshape(b, 1, d) for i in range(6)]
        w = _prep_layer(w_in[l], b_fox_f[l], b_m_i[l], b_m_f[l], g_q[l], g_k[l], g_m_out[l],
                        g_sgu[l], w_s[l], b_s[l])
        qT, k, kb, vT, qkvm, osig, ysgu, gn, gT = _in_projection(x, sh1, sc1, g_mix[l].reshape(1, d), w)
        ya = _fox_attention(qT, k, kb, vT)
        ym = _mlstm(qkvm, osig, gn, gT, w["e256"], w["gmo"])
        x = _post(x, ya, ym, ysgu, gt1, sh2, sc2, gt2, g_ffn[l].reshape(1, d),
                  w_out[l].astype(BF16), w_up[l].astype(BF16), w_conv[l],
                  b_conv[l].reshape(1, 2 * D_FF), w_down[l].astype(BF16))
    return x
```

```python
import functools

import jax
import jax.numpy as jnp
from jax import lax
from jax.experimental import pallas as pl
from jax.experimental.pallas import tpu as pltpu

F32 = jnp.float32
BF16 = jnp.bfloat16

D_MODEL = 1024
HEAD_DIM = 64
D_ATTN = 512
D_MLSTM = 256
D_SGU = 256
N_ATTN_HEADS = 8
N_MLSTM_HEADS = 4
N_SGU_GROUPS = 4
SGU_CHUNK = 128
D_FF = 2816
EPS = 1e-6
LOG2E = 1.4426950408889634

LANES = 128
IN_TILE = 512
ATTN_TILE = 512
BIAS_PARTS = 3
ONES_ROWS = 16
MLSTM_CHUNK = 256
POST_TILE = 512
FF_CHUNK = 256
GATE_LANE_M = 8
NEG_BIG = -1e30
VMEM_LIMIT = 56 * 1024 * 1024


def _nt_dot(a, b):
    return lax.dot_general(a, b, (((1,), (1,)), ((), ())), preferred_element_type=F32)


def _tn_dot(a, b):
    return lax.dot_general(a, b, (((0,), (0,)), ((), ())), preferred_element_type=F32)


def _dot(a, b):
    return jnp.dot(a, b, preferred_element_type=F32)


def _split_dot(x, e):
    hi = x.astype(BF16)
    lo = (x - hi.astype(F32)).astype(BF16)
    return _dot(hi, e) + _dot(lo, e)


def _scan_rows(x, op, identity):
    n = x.shape[0]
    row = lax.broadcasted_iota(jnp.int32, x.shape, 0)
    k = 1
    while k < n:
        shifted = pltpu.roll(x, k, axis=0)
        x = op(x, jnp.where(row >= k, shifted, identity))
        k *= 2
    return x


def _mod_kernel(c_ref, w_ref, b_ref, o_ref):
    c = c_ref[...]
    ca = c * jax.nn.sigmoid(c)
    o_ref[0] = jnp.dot(ca, w_ref[0], preferred_element_type=F32,
                       precision=lax.Precision.HIGHEST) + b_ref[0]


def _modulation(c, w_ada, b_ada):
    depth, d, n = w_ada.shape
    b = c.shape[0]
    tn = 1536
    return pl.pallas_call(
        _mod_kernel,
        grid=(depth, n // tn),
        in_specs=[
            pl.BlockSpec((b, d), lambda l, j: (0, 0)),
            pl.BlockSpec((1, d, tn), lambda l, j: (l, 0, j)),
            pl.BlockSpec((1, 1, tn), lambda l, j: (l, 0, j)),
        ],
        out_specs=pl.BlockSpec((1, b, tn), lambda l, j: (l, 0, j)),
        out_shape=jax.ShapeDtypeStruct((depth, b, n), F32),
        compiler_params=pltpu.CompilerParams(
            dimension_semantics=("arbitrary", "arbitrary"), vmem_limit_bytes=VMEM_LIMIT),
        name="modulation",
    )(c, w_ada, b_ada.reshape(depth, 1, n))


def _in_kernel(x_ref, sh_ref, sc_ref, g_ref, wqT_ref, wk_ref, wvT_ref, wm_ref, ws_ref, wg_ref,
               gq_ref, gk_ref, e512_ref, e256_ref, gsgu_ref, wst_ref, bst_ref, gbias_ref, place_ref,
               qT_ref, k_ref, kb_ref, vT_ref, qkvm_ref, osig_ref, ysgu_ref, gn_ref, gT_ref,
               carry_ref):
    tm = x_ref.shape[1]

    @pl.when(pl.program_id(1) == 0)
    def _():
        carry_ref[...] = jnp.zeros_like(carry_ref)

    x = x_ref[0]
    ms = jnp.mean(x * x, axis=-1, keepdims=True)
    h = x * lax.rsqrt(ms + EPS) * g_ref[...] * (1.0 + sc_ref[0]) + sh_ref[0]
    hb = h.astype(BF16)

    qT = _nt_dot(wqT_ref[...], hb)
    q_scale = (HEAD_DIM ** -0.5) * LOG2E
    for hd in range(N_ATTN_HEADS):
        blk = qT[hd * HEAD_DIM:(hd + 1) * HEAD_DIM]
        msq = jnp.mean(blk * blk, axis=0, keepdims=True)
        qn = blk * lax.rsqrt(msq + EPS) * (gq_ref[...] * q_scale)
        qT_ref[0, hd * HEAD_DIM:(hd + 1) * HEAD_DIM, :] = qn.astype(BF16)

    kk = _dot(hb, wk_ref[...])
    msk = _dot((kk * kk).astype(BF16), e512_ref[...])
    k_ref[0] = (kk * lax.rsqrt(msk + EPS) * gk_ref[...]).astype(BF16)

    vT_ref[0] = _nt_dot(wvT_ref[...], hb).astype(BF16)

    pm = _dot(hb, wm_ref[...])
    qkvm_ref[0, :, 0:D_MLSTM] = pm[:, 0:D_MLSTM].astype(BF16)
    qkvm_ref[0, :, D_MLSTM:2 * D_MLSTM] = (pm[:, D_MLSTM:2 * D_MLSTM] * HEAD_DIM ** -0.5).astype(BF16)
    qkvm_ref[0, :, 2 * D_MLSTM:3 * D_MLSTM] = pm[:, 2 * D_MLSTM:3 * D_MLSTM].astype(BF16)
    osig_ref[0] = jax.nn.sigmoid(pm[:, 3 * D_MLSTM:4 * D_MLSTM])

    pg = _dot(hb, wg_ref[...]) + gbias_ref[...]
    lane = lax.broadcasted_iota(jnp.int32, (tm, LANES), 1)
    ls = jnp.where(lane < GATE_LANE_M + N_MLSTM_HEADS, jax.nn.log_sigmoid(pg[:, 0:LANES]), 0.0)
    cs = _scan_rows(ls, jnp.add, 0.0) + carry_ref[0:1, :]
    a = pg[:, LANES:2 * LANES] - cs
    cm = jnp.maximum(_scan_rows(a, jnp.maximum, -jnp.inf), carry_ref[8:9, :])
    carry_ref[0:1, :] = cs[tm - 1:tm, :]
    carry_ref[8:9, :] = cm[tm - 1:tm, :]
    nf = -LOG2E * cs
    hi = nf.astype(BF16)
    r1 = nf - hi.astype(F32)
    mid = r1.astype(BF16)
    lo = (r1 - mid.astype(F32)).astype(BF16)
    kb_ref[0] = _dot(jnp.concatenate([hi, mid, lo], axis=1), place_ref[...]).astype(BF16)
    g1 = jnp.where(lane < GATE_LANE_M, nf, a)
    gn_ref[0, :, 0:LANES] = g1
    gn_ref[0, :, LANES:2 * LANES] = cm
    gn_ref[0, :, 2 * LANES:3 * LANES] = cs + cm
    gT_ref[0] = g1.T

    ps = _dot(hb, ws_ref[...])
    gu = jax.nn.gelu(ps[:, 0:D_SGU])
    gv = jax.nn.gelu(ps[:, D_SGU:2 * D_SGU])
    e256 = e256_ref[...]
    xc = gv - _split_dot(gv, e256)
    var = _split_dot(xc * xc, e256)
    vnb = (xc * lax.rsqrt(var + EPS) * gsgu_ref[...]).astype(BF16)
    nrow = N_SGU_GROUPS * SGU_CHUNK
    wrow = lax.broadcasted_iota(jnp.int32, (nrow, SGU_CHUNK), 0)
    wcol = lax.broadcasted_iota(jnp.int32, (nrow, SGU_CHUNK), 1)
    wst = jnp.where((wrow % SGU_CHUNK) >= wcol, wst_ref[...], 0.0).astype(BF16)
    grp = lax.broadcasted_iota(jnp.int32, (SGU_CHUNK, D_SGU), 1) // HEAD_DIM
    for c in range(tm // SGU_CHUNK):
        r = _dot(wst, vnb[c * SGU_CHUNK:(c + 1) * SGU_CHUNK])
        mixed = r[0:SGU_CHUNK]
        for gidx in range(1, N_SGU_GROUPS):
            mixed = jnp.where(grp == gidx, r[gidx * SGU_CHUNK:(gidx + 1) * SGU_CHUNK], mixed)
        ysgu_ref[0, c * SGU_CHUNK:(c + 1) * SGU_CHUNK, :] = (
            gu[c * SGU_CHUNK:(c + 1) * SGU_CHUNK] * (mixed + bst_ref[...])).astype(BF16)


def _const_spec(shape):
    nd = len(shape)
    return pl.BlockSpec(shape, lambda b, s: (0,) * nd, pipeline_mode=pl.Buffered(1))


def _in_projection(x, sh, sc, g, w):
    b, s, d = x.shape
    tm = IN_TILE
    tok = lambda width: pl.BlockSpec((1, tm, width), lambda bi, si: (bi, si, 0))
    tokT = lambda rows: pl.BlockSpec((1, rows, tm), lambda bi, si: (bi, 0, si))
    mod = pl.BlockSpec((1, 1, d), lambda bi, si: (bi, 0, 0))
    consts = [g, w["wqT"], w["wk"], w["wvT"], w["wm"], w["ws"], w["wg"], w["gq"], w["gk"],
              w["e512"], w["e256"], w["gsgu"], w["wst"], w["bst"], w["gbias"], w["place"]]
    return pl.pallas_call(
        _in_kernel,
        grid=(b, s // tm),
        in_specs=[tok(d), mod, mod] + [_const_spec(c.shape) for c in consts],
        out_specs=[tokT(D_ATTN), tok(D_ATTN), tok(D_ATTN), tokT(D_ATTN), tok(3 * D_MLSTM),
                   tok(D_MLSTM), tok(D_SGU), tok(3 * LANES), tokT(LANES)],
        out_shape=[
            jax.ShapeDtypeStruct((b, D_ATTN, s), BF16),
            jax.ShapeDtypeStruct((b, s, D_ATTN), BF16),
            jax.ShapeDtypeStruct((b, s, D_ATTN), BF16),
            jax.ShapeDtypeStruct((b, D_ATTN, s), BF16),
            jax.ShapeDtypeStruct((b, s, 3 * D_MLSTM), BF16),
            jax.ShapeDtypeStruct((b, s, D_MLSTM), F32),
            jax.ShapeDtypeStruct((b, s, D_SGU), BF16),
            jax.ShapeDtypeStruct((b, s, 3 * LANES), F32),
            jax.ShapeDtypeStruct((b, LANES, s), F32),
        ],
        scratch_shapes=[pltpu.VMEM((16, LANES), F32)],
        compiler_params=pltpu.CompilerParams(
            dimension_semantics=("arbitrary", "arbitrary"), vmem_limit_bytes=VMEM_LIMIT),
        name="in_projection",
    )(x, sh, sc, *consts)


def _attn_kernel(qT_ref, qTn_ref, k_ref, kb_ref, vT_ref, o_ref, m_ref, bm_ref, acc_ref, s_ref):
    tq = qT_ref.shape[2]
    tk = tq
    qi = pl.program_id(2)

    def score_rhs(qT):
        row = lax.broadcasted_iota(jnp.int32, qT.shape, 0)
        out = []
        for j in range(2):
            qp = jnp.where((row >= j * HEAD_DIM) & (row < (j + 1) * HEAD_DIM), qT, jnp.zeros_like(qT))
            sel = jnp.where((row >= BIAS_PARTS * j) & (row < BIAS_PARTS * (j + 1)), 1.0, 0.0).astype(BF16)
            out.append(jnp.concatenate([qp, sel], axis=0))
        return out

    rhs = score_rhs(qT_ref[0])

    m_ref[...] = jnp.full(m_ref.shape, NEG_BIG, F32)
    acc_ref[...] = jnp.zeros(acc_ref.shape, F32)
    ones = jnp.ones((ONES_ROWS, tk), BF16)

    def scores(kb, j, masked, rhs=rhs):
        off = pl.multiple_of(kb * tk, tk)
        lhs = jnp.concatenate([k_ref[0, pl.ds(off, tk), :], kb_ref[0, pl.ds(off, tk), :]], axis=1)
        s = _dot(lhs, rhs[j])
        if masked:
            kidx = lax.broadcasted_iota(jnp.int32, s.shape, 0)
            qidx = lax.broadcasted_iota(jnp.int32, s.shape, 1)
            s = jnp.where(kidx <= qidx, s, NEG_BIG)
        s_ref[j] = s
        bm_ref[j, 0:1, :] = jnp.max(s, axis=0, keepdims=True)

    def probs(j):
        m_old = m_ref[j, 0:1, :]
        m_new = jnp.maximum(m_old, bm_ref[j, 0:1, :])
        m_ref[j, 0:1, :] = m_new
        return jnp.exp2(s_ref[j] - m_new).astype(BF16), jnp.exp2(m_old - m_new)

    def accumulate(kb, j, p, alpha):
        off = pl.multiple_of(kb * tk, tk)
        vt = jnp.concatenate(
            [vT_ref[0, j * HEAD_DIM:(j + 1) * HEAD_DIM, pl.ds(off, tk)], ones], axis=0)
        acc_ref[j] = alpha * acc_ref[j] + _dot(vt, p)

    def pipelined(kb, next_masked):
        for j in range(2):
            p, alpha = probs(j)
            scores(kb + 1, j, next_masked)
            accumulate(kb, j, p, alpha)

    def finish():
        rhs_next = score_rhs(qTn_ref[0])
        for j in range(2):
            p, alpha = probs(j)
            scores(0, j, False, rhs_next)
            accumulate(qi, j, p, alpha)
        outT = jnp.concatenate(
            [acc_ref[j, 0:HEAD_DIM, :] / acc_ref[j, HEAD_DIM:HEAD_DIM + 1, :] for j in range(2)],
            axis=0)
        o_ref[0] = outT.T.astype(BF16)

    @pl.when(qi == 0)
    def _():
        for j in range(2):
            scores(0, j, True)
        finish()

    @pl.when(qi > 0)
    def _():
        n_plain = qi - 1

        def body(i, carry):
            pipelined(2 * i, False)
            pipelined(2 * i + 1, False)
            return carry

        lax.fori_loop(0, lax.shift_right_logical(n_plain, 1), body, 0)

        @pl.when((n_plain & 1) == 1)
        def _():
            pipelined(n_plain - 1, False)

        pipelined(qi - 1, True)
        finish()


def _fox_attention(qT, k, kb, vT):
    b, s, _ = k.shape
    tq = ATTN_TILE
    pairs = N_ATTN_HEADS // 2
    nq = s // tq
    return pl.pallas_call(
        _attn_kernel,
        grid=(b, pairs, nq),
        in_specs=[
            pl.BlockSpec((1, 2 * HEAD_DIM, tq), lambda bi, p, qi: (bi, p, qi)),
            pl.BlockSpec((1, 2 * HEAD_DIM, tq), lambda bi, p, qi: (bi, p, jnp.minimum(qi + 1, nq - 1))),
            pl.BlockSpec((1, s, 2 * HEAD_DIM), lambda bi, p, qi: (bi, 0, p)),
            pl.BlockSpec((1, s, LANES), lambda bi, p, qi: (bi, 0, p)),
            pl.BlockSpec((1, 2 * HEAD_DIM, s), lambda bi, p, qi: (bi, p, 0)),
        ],
        out_specs=pl.BlockSpec((1, tq, 2 * HEAD_DIM), lambda bi, p, qi: (bi, qi, p)),
        out_shape=jax.ShapeDtypeStruct((b, s, D_ATTN), BF16),
        scratch_shapes=[pltpu.VMEM((2, 8, tq), F32),
                        pltpu.VMEM((2, 8, tq), F32),
                        pltpu.VMEM((2, HEAD_DIM + ONES_ROWS, tq), F32),
                        pltpu.VMEM((2, tq, tq), F32)],
        compiler_params=pltpu.CompilerParams(
            dimension_semantics=("arbitrary", "arbitrary", "arbitrary"),
            vmem_limit_bytes=VMEM_LIMIT),
        name="fox_attention",
    )(qT, qT, k, kb, vT)


def _mlstm_kernel(qkv_ref, osig_ref, gn_ref, gT_ref, e256_ref, gmo_ref, o_ref, cn_ref, acs_ref):
    L = qkv_ref.shape[1]
    dm = D_MLSTM

    @pl.when(pl.program_id(1) == 0)
    def _():
        cn_ref[...] = jnp.zeros_like(cn_ref)
        acs_ref[...] = jnp.zeros_like(acs_ref)

    q = qkv_ref[0, :, 0:dm]
    k = qkv_ref[0, :, dm:2 * dm]
    v = qkv_ref[0, :, 2 * dm:3 * dm]
    a_nat = gn_ref[0, :, 0:LANES]
    A_nat = gn_ref[0, :, LANES:2 * LANES]
    m_nat = gn_ref[0, :, 2 * LANES:3 * LANES]
    aT = gT_ref[0]
    A_prev = acs_ref[0:1, :]
    A_end = A_nat[L - 1:L, :]

    grp = lax.broadcasted_iota(jnp.int32, (L, dm), 1) // HEAD_DIM
    causal = (lax.broadcasted_iota(jnp.int32, (L, L), 0) >= lax.broadcasted_iota(jnp.int32, (L, L), 1))
    qcn = _dot(q, cn_ref[...].astype(BF16))
    zero_q = jnp.zeros_like(q)
    hout = jnp.zeros((L, dm), F32)
    wst = jnp.zeros((L, dm), F32)
    for hd in range(N_MLSTM_HEADS):
        gl = GATE_LANE_M + hd
        a_row = aT[gl:gl + 1, :]
        A_col = A_nat[:, gl:gl + 1]
        w = jnp.exp(jnp.where(causal, a_row - A_col, -jnp.inf))
        sw = _nt_dot(jnp.where(grp == hd, q, zero_q), k) * w
        den_intra = jnp.sum(sw, axis=1, keepdims=True)
        pv = _dot(sw.astype(BF16), v)
        e_col = jnp.exp(A_prev[:, gl:gl + 1] - A_col)
        den = den_intra + e_col * qcn[:, dm + hd:dm + hd + 1]
        r = 1.0 / jnp.maximum(jnp.abs(den), jnp.exp(-m_nat[:, gl:gl + 1]))
        hout = jnp.where(grp == hd, (pv + e_col * qcn[:, 0:dm]) * r, hout)
        wst = jnp.where(grp == hd, jnp.exp(a_nat[:, gl:gl + 1] - A_end[:, gl:gl + 1]), wst)

    msq = _split_dot(hout * hout, e256_ref[...])
    hn = hout * lax.rsqrt(msq + EPS) * gmo_ref[...]
    o_ref[0] = (osig_ref[0] * hn).astype(BF16)

    kw = (k.astype(F32) * wst).astype(BF16)
    rhs = jnp.concatenate([v, jnp.ones((L, LANES), BF16)], axis=1)
    upd = _tn_dot(kw, rhs)
    rowg = lax.broadcasted_iota(jnp.int32, upd.shape, 0) // HEAD_DIM
    col = lax.broadcasted_iota(jnp.int32, upd.shape, 1)
    keep = jnp.where(col < dm, col // HEAD_DIM, col - dm) == rowg
    dvec = jnp.exp(A_prev - A_end)
    decay = jnp.zeros(upd.shape, F32)
    for hd in range(N_MLSTM_HEADS):
        gl = GATE_LANE_M + hd
        decay = jnp.where(rowg == hd, dvec[:, gl:gl + 1], decay)
    cn_ref[...] = decay * cn_ref[...] + jnp.where(keep, upd, 0.0)
    acs_ref[0:1, :] = A_end


def _mlstm(qkvm, osig, gn, gT, e256, gmo):
    b, s, _ = qkvm.shape
    L = MLSTM_CHUNK
    return pl.pallas_call(
        _mlstm_kernel,
        grid=(b, s // L),
        in_specs=[
            pl.BlockSpec((1, L, 3 * D_MLSTM), lambda bi, ci: (bi, ci, 0)),
            pl.BlockSpec((1, L, D_MLSTM), lambda bi, ci: (bi, ci, 0)),
            pl.BlockSpec((1, L, 3 * LANES), lambda bi, ci: (bi, ci, 0)),
            pl.BlockSpec((1, LANES, L), lambda bi, ci: (bi, 0, ci)),
            _const_spec(e256.shape),
            _const_spec(gmo.shape),
        ],
        out_specs=pl.BlockSpec((1, L, D_MLSTM), lambda bi, ci: (bi, ci, 0)),
        out_shape=jax.ShapeDtypeStruct((b, s, D_MLSTM), BF16),
        scratch_shapes=[pltpu.VMEM((D_MLSTM, D_MLSTM + LANES), F32), pltpu.VMEM((8, LANES), F32)],
        compiler_params=pltpu.CompilerParams(
            dimension_semantics=("arbitrary", "arbitrary"), vmem_limit_bytes=VMEM_LIMIT),
        name="mlstm",
    )(qkvm, osig, gn, gT, e256, gmo)


def _post_kernel(x_ref, ya_ref, ym_ref, ys_ref, gt1_ref, sh2_ref, sc2_ref, gt2_ref, g_ref,
                 wo_ref, wup_ref, wconv_ref, bconv_ref, wdown_ref, o_ref, halo_ref, hact_ref):
    tm = x_ref.shape[1]

    @pl.when(pl.program_id(1) == 0)
    def _():
        halo_ref[...] = jnp.zeros_like(halo_ref)

    y = (_dot(ya_ref[0], wo_ref[0:D_ATTN, :])
         + _dot(ym_ref[0], wo_ref[D_ATTN:D_ATTN + D_MLSTM, :])
         + _dot(ys_ref[0], wo_ref[D_ATTN + D_MLSTM:D_MODEL, :]))
    x1 = x_ref[0] + gt1_ref[0] * y
    ms = jnp.mean(x1 * x1, axis=-1, keepdims=True)
    h2 = (x1 * lax.rsqrt(ms + EPS) * g_ref[...] * (1.0 + sc2_ref[0]) + sh2_ref[0]).astype(BF16)

    row = lax.broadcasted_iota(jnp.int32, (tm, FF_CHUNK), 0)

    def conv(cols):
        z = _dot(h2, wup_ref[:, cols])
        prev2 = halo_ref[6:7, cols]
        prev1 = halo_ref[7:8, cols]
        halo_ref[:, cols] = z[tm - 8:tm, :]
        z1 = jnp.where(row == 0, prev1, pltpu.roll(z, 1, axis=0))
        z2 = jnp.where(row == 0, prev2, jnp.where(row == 1, prev1, pltpu.roll(z, 2, axis=0)))
        return (wconv_ref[0:1, cols] * z2 + wconv_ref[1:2, cols] * z1 + wconv_ref[2:3, cols] * z
                + bconv_ref[:, cols])

    for c in range(D_FF // FF_CHUNK):
        a = conv(pl.ds(c * FF_CHUNK, FF_CHUNK))
        gv = conv(pl.ds(D_FF + c * FF_CHUNK, FF_CHUNK))
        hact_ref[:, c * FF_CHUNK:(c + 1) * FF_CHUNK] = (a * jax.nn.sigmoid(a) * gv).astype(BF16)

    o_ref[0] = x1 + gt2_ref[0] * _dot(hact_ref[...], wdown_ref[...])


def _post(x, ya, ym, ys, gt1, sh2, sc2, gt2, g, wo, wup, wconv, bconv, wdown):
    b, s, d = x.shape
    tm = POST_TILE
    tok = lambda width: pl.BlockSpec((1, tm, width), lambda bi, si: (bi, si, 0))
    mod = pl.BlockSpec((1, 1, d), lambda bi, si: (bi, 0, 0))
    consts = [g, wo, wup, wconv, bconv, wdown]
    return pl.pallas_call(
        _post_kernel,
        grid=(b, s // tm),
        in_specs=[tok(d), tok(D_ATTN), tok(D_MLSTM), tok(D_SGU), mod, mod, mod, mod]
        + [_const_spec(c.shape) for c in consts],
        out_specs=tok(d),
        out_shape=jax.ShapeDtypeStruct((b, s, d), F32),
        scratch_shapes=[pltpu.VMEM((8, 2 * D_FF), F32), pltpu.VMEM((tm, D_FF), BF16)],
        compiler_params=pltpu.CompilerParams(
            dimension_semantics=("arbitrary", "arbitrary"), vmem_limit_bytes=VMEM_LIMIT),
        name="out_proj_ffn",
    )(x, ya, ym, ys, gt1, sh2, sc2, gt2, *consts)


def _block_diag_mean(n):
    idx = jnp.arange(n) // HEAD_DIM
    return jnp.where(idx[:, None] == idx[None, :], 1.0 / HEAD_DIM, 0.0).astype(BF16)


def _prep_layer(w_in, b_fox_f, b_m_i, b_m_f, g_q, g_k, g_m_out, g_sgu, w_s, b_s):
    o = 0
    wq = w_in[:, o:o + D_ATTN]; o += D_ATTN
    wk = w_in[:, o:o + D_ATTN]; o += D_ATTN
    wv = w_in[:, o:o + D_ATTN]; o += D_ATTN
    wm = w_in[:, o:o + 4 * D_MLSTM]; o += 4 * D_MLSTM
    ws = w_in[:, o:o + 2 * D_SGU]; o += 2 * D_SGU
    wfa = w_in[:, o:o + N_ATTN_HEADS]; o += N_ATTN_HEADS
    wim = w_in[:, o:o + N_MLSTM_HEADS]; o += N_MLSTM_HEADS
    wfm = w_in[:, o:o + N_MLSTM_HEADS]
    lo, hi = GATE_LANE_M, GATE_LANE_M + N_MLSTM_HEADS
    wg = jnp.zeros((D_MODEL, 2 * LANES), F32)
    wg = wg.at[:, 0:N_ATTN_HEADS].set(wfa).at[:, lo:hi].set(wfm).at[:, LANES + lo:LANES + hi].set(wim)
    gbias = jnp.zeros((1, 2 * LANES), F32)
    gbias = (gbias.at[0, 0:N_ATTN_HEADS].set(b_fox_f).at[0, lo:hi].set(b_m_f)
             .at[0, LANES + lo:LANES + hi].set(b_m_i))
    heads = jnp.arange(N_ATTN_HEADS)
    place = jnp.zeros((BIAS_PARTS * LANES, D_ATTN), F32)
    for r in range(BIAS_PARTS):
        place = place.at[r * LANES + heads, (heads // 2) * LANES + BIAS_PARTS * (heads % 2) + r].set(1.0)
    return dict(
        place=place.astype(BF16),
        wqT=wq.T.astype(BF16), wk=wk.astype(BF16), wvT=wv.T.astype(BF16), wm=wm.astype(BF16),
        ws=ws.astype(BF16), wg=wg.astype(BF16), gbias=gbias,
        gq=g_q.reshape(HEAD_DIM, 1), gk=jnp.tile(g_k, N_ATTN_HEADS).reshape(1, D_ATTN),
        e512=_block_diag_mean(D_ATTN), e256=_block_diag_mean(D_SGU),
        gsgu=g_sgu.reshape(1, D_SGU), wst=w_s.reshape(N_SGU_GROUPS * SGU_CHUNK, SGU_CHUNK),
        bst=jnp.repeat(b_s.T, HEAD_DIM, axis=1), gmo=g_m_out.reshape(1, D_MLSTM),
    )


def kernel(x, c, w_ada, b_ada, g_mix, g_ffn, w_in, b_fox_f, b_m_i, b_m_f, g_q, g_k, g_m_out, g_sgu, w_s, b_s, w_out, w_up, w_conv, b_conv, w_down):
    depth = w_ada.shape[0]
    b, s, d = x.shape
    assert d == D_MODEL and s % max(IN_TILE, POST_TILE, MLSTM_CHUNK, ATTN_TILE) == 0
    mod = _modulation(c, w_ada, b_ada)
    for l in range(depth):
        sh1, sc1, gt1, sh2, sc2, gt2 = [mod[l, :, i * d:(i + 1) * d].reshape(b, 1, d) for i in range(6)]
        w = _prep_layer(w_in[l], b_fox_f[l], b_m_i[l], b_m_f[l], g_q[l], g_k[l], g_m_out[l],
                        g_sgu[l], w_s[l], b_s[l])
        qT, k, kb, vT, qkvm, osig, ysgu, gn, gT = _in_projection(x, sh1, sc1, g_mix[l].reshape(1, d), w)
        ya = _fox_attention(qT, k, kb, vT)
        ym = _mlstm(qkvm, osig, gn, gT, w["e256"], w["gmo"])
        x = _post(x, ya, ym, ysgu, gt1, sh2, sc2, gt2, g_ffn[l].reshape(1, d),
                  w_out[l].astype(BF16), w_up[l].astype(BF16), w_conv[l],
                  b_conv[l].reshape(1, 2 * D_FF), w_down[l].astype(BF16))
    return x
```

```python
import functools

import jax
import jax.numpy as jnp
from jax import lax
from jax.experimental import pallas as pl
from jax.experimental.pallas import tpu as pltpu

F32 = jnp.float32
BF16 = jnp.bfloat16

D_MODEL = 1024
HEAD_DIM = 64
D_ATTN = 512
D_MLSTM = 256
D_SGU = 256
N_ATTN_HEADS = 8
N_MLSTM_HEADS = 4
N_SGU_GROUPS = 4
SGU_CHUNK = 128
D_FF = 2816
EPS = 1e-6
LOG2E = 1.4426950408889634

LANES = 128
IN_TILE = 512
ATTN_TILE = 512
BIAS_PARTS = 3
ONES_ROWS = 16
MLSTM_CHUNK = 512
POST_TILE = 512
FF_CHUNK = 256
GATE_LANE_M = 8
NEG_BIG = -1e30
VMEM_LIMIT = 56 * 1024 * 1024


def _nt_dot(a, b):
    return lax.dot_general(a, b, (((1,), (1,)), ((), ())), preferred_element_type=F32)


def _tn_dot(a, b):
    return lax.dot_general(a, b, (((0,), (0,)), ((), ())), preferred_element_type=F32)


def _dot(a, b):
    return jnp.dot(a, b, preferred_element_type=F32)


def _split_dot(x, e):
    hi = x.astype(BF16)
    lo = (x - hi.astype(F32)).astype(BF16)
    return _dot(hi, e) + _dot(lo, e)


def _scan_rows(x, op, identity):
    n = x.shape[0]
    row = lax.broadcasted_iota(jnp.int32, x.shape, 0)
    k = 1
    while k < n:
        shifted = pltpu.roll(x, k, axis=0)
        x = op(x, jnp.where(row >= k, shifted, identity))
        k *= 2
    return x


def _mod_kernel(c_ref, w_ref, b_ref, o_ref):
    c = c_ref[...]
    ca = c * jax.nn.sigmoid(c)
    o_ref[0] = jnp.dot(ca, w_ref[0], preferred_element_type=F32,
                       precision=lax.Precision.HIGHEST) + b_ref[0]


def _modulation(c, w_ada, b_ada):
    depth, d, n = w_ada.shape
    b = c.shape[0]
    tn = 1536
    return pl.pallas_call(
        _mod_kernel,
        grid=(depth, n // tn),
        in_specs=[
            pl.BlockSpec((b, d), lambda l, j: (0, 0)),
            pl.BlockSpec((1, d, tn), lambda l, j: (l, 0, j)),
            pl.BlockSpec((1, 1, tn), lambda l, j: (l, 0, j)),
        ],
        out_specs=pl.BlockSpec((1, b, tn), lambda l, j: (l, 0, j)),
        out_shape=jax.ShapeDtypeStruct((depth, b, n), F32),
        compiler_params=pltpu.CompilerParams(
            dimension_semantics=("arbitrary", "arbitrary"), vmem_limit_bytes=VMEM_LIMIT),
        name="modulation",
    )(c, w_ada, b_ada.reshape(depth, 1, n))


def _in_kernel(x_ref, sh_ref, sc_ref, g_ref, wqT_ref, wk_ref, wvT_ref, wm_ref, ws_ref, wg_ref,
               gq_ref, gk_ref, e512_ref, e256_ref, gsgu_ref, wst_ref, bst_ref, gbias_ref, place_ref,
               qT_ref, k_ref, kb_ref, vT_ref, qkvm_ref, osig_ref, ysgu_ref, gn_ref, gT_ref,
               carry_ref):
    tm = x_ref.shape[1]

    @pl.when(pl.program_id(1) == 0)
    def _():
        carry_ref[...] = jnp.zeros_like(carry_ref)

    x = x_ref[0]
    ms = jnp.mean(x * x, axis=-1, keepdims=True)
    h = x * lax.rsqrt(ms + EPS) * g_ref[...] * (1.0 + sc_ref[0]) + sh_ref[0]
    hb = h.astype(BF16)

    qT = _nt_dot(wqT_ref[...], hb)
    q_scale = (HEAD_DIM ** -0.5) * LOG2E
    for hd in range(N_ATTN_HEADS):
        blk = qT[hd * HEAD_DIM:(hd + 1) * HEAD_DIM]
        msq = jnp.mean(blk * blk, axis=0, keepdims=True)
        qn = blk * lax.rsqrt(msq + EPS) * (gq_ref[...] * q_scale)
        qT_ref[0, hd * HEAD_DIM:(hd + 1) * HEAD_DIM, :] = qn.astype(BF16)

    kk = _dot(hb, wk_ref[...])
    msk = _dot((kk * kk).astype(BF16), e512_ref[...])
    k_ref[0] = (kk * lax.rsqrt(msk + EPS) * gk_ref[...]).astype(BF16)

    vT_ref[0] = _nt_dot(wvT_ref[...], hb).astype(BF16)

    pm = _dot(hb, wm_ref[...])
    qkvm_ref[0, :, 0:D_MLSTM] = pm[:, 0:D_MLSTM].astype(BF16)
    qkvm_ref[0, :, D_MLSTM:2 * D_MLSTM] = (pm[:, D_MLSTM:2 * D_MLSTM] * HEAD_DIM ** -0.5).astype(BF16)
    qkvm_ref[0, :, 2 * D_MLSTM:3 * D_MLSTM] = pm[:, 2 * D_MLSTM:3 * D_MLSTM].astype(BF16)
    osig_ref[0] = jax.nn.sigmoid(pm[:, 3 * D_MLSTM:4 * D_MLSTM])

    pg = _dot(hb, wg_ref[...]) + gbias_ref[...]
    lane = lax.broadcasted_iota(jnp.int32, (tm, LANES), 1)
    ls = jnp.where(lane < GATE_LANE_M + N_MLSTM_HEADS, jax.nn.log_sigmoid(pg[:, 0:LANES]), 0.0)
    cs = _scan_rows(ls, jnp.add, 0.0) + carry_ref[0:1, :]
    a = pg[:, LANES:2 * LANES] - cs
    cm = jnp.maximum(_scan_rows(a, jnp.maximum, -jnp.inf), carry_ref[8:9, :])
    carry_ref[0:1, :] = cs[tm - 1:tm, :]
    carry_ref[8:9, :] = cm[tm - 1:tm, :]
    nf = -LOG2E * cs
    hi = nf.astype(BF16)
    r1 = nf - hi.astype(F32)
    mid = r1.astype(BF16)
    lo = (r1 - mid.astype(F32)).astype(BF16)
    kb_ref[0] = _dot(jnp.concatenate([hi, mid, lo], axis=1), place_ref[...]).astype(BF16)
    g1 = jnp.where(lane < GATE_LANE_M, nf, a)
    gn_ref[0, :, 0:LANES] = g1
    gn_ref[0, :, LANES:2 * LANES] = cm
    gn_ref[0, :, 2 * LANES:3 * LANES] = cs + cm
    gT_ref[0] = g1.T

    ps = _dot(hb, ws_ref[...])
    gu = jax.nn.gelu(ps[:, 0:D_SGU])
    gv = jax.nn.gelu(ps[:, D_SGU:2 * D_SGU])
    e256 = e256_ref[...]
    xc = gv - _split_dot(gv, e256)
    var = _split_dot(xc * xc, e256)
    vnb = (xc * lax.rsqrt(var + EPS) * gsgu_ref[...]).astype(BF16)
    nrow = N_SGU_GROUPS * SGU_CHUNK
    wrow = lax.broadcasted_iota(jnp.int32, (nrow, SGU_CHUNK), 0)
    wcol = lax.broadcasted_iota(jnp.int32, (nrow, SGU_CHUNK), 1)
    wst = jnp.where((wrow % SGU_CHUNK) >= wcol, wst_ref[...], 0.0).astype(BF16)
    grp = lax.broadcasted_iota(jnp.int32, (SGU_CHUNK, D_SGU), 1) // HEAD_DIM
    for c in range(tm // SGU_CHUNK):
        r = _dot(wst, vnb[c * SGU_CHUNK:(c + 1) * SGU_CHUNK])
        mixed = r[0:SGU_CHUNK]
        for gidx in range(1, N_SGU_GROUPS):
            mixed = jnp.where(grp == gidx, r[gidx * SGU_CHUNK:(gidx + 1) * SGU_CHUNK], mixed)
        ysgu_ref[0, c * SGU_CHUNK:(c + 1) * SGU_CHUNK, :] = (
            gu[c * SGU_CHUNK:(c + 1) * SGU_CHUNK] * (mixed + bst_ref[...])).astype(BF16)


def _const_spec(shape):
    nd = len(shape)
    return pl.BlockSpec(shape, lambda b, s: (0,) * nd, pipeline_mode=pl.Buffered(1))


def _in_projection(x, sh, sc, g, w):
    b, s, d = x.shape
    tm = IN_TILE
    tok = lambda width: pl.BlockSpec((1, tm, width), lambda bi, si: (bi, si, 0))
    tokT = lambda rows: pl.BlockSpec((1, rows, tm), lambda bi, si: (bi, 0, si))
    mod = pl.BlockSpec((1, 1, d), lambda bi, si: (bi, 0, 0))
    consts = [g, w["wqT"], w["wk"], w["wvT"], w["wm"], w["ws"], w["wg"], w["gq"], w["gk"],
              w["e512"], w["e256"], w["gsgu"], w["wst"], w["bst"], w["gbias"], w["place"]]
    return pl.pallas_call(
        _in_kernel,
        grid=(b, s // tm),
        in_specs=[tok(d), mod, mod] + [_const_spec(c.shape) for c in consts],
        out_specs=[tokT(D_ATTN), tok(D_ATTN), tok(D_ATTN), tokT(D_ATTN), tok(3 * D_MLSTM),
                   tok(D_MLSTM), tok(D_SGU), tok(3 * LANES), tokT(LANES)],
        out_shape=[
            jax.ShapeDtypeStruct((b, D_ATTN, s), BF16),
            jax.ShapeDtypeStruct((b, s, D_ATTN), BF16),
            jax.ShapeDtypeStruct((b, s, D_ATTN), BF16),
            jax.ShapeDtypeStruct((b, D_ATTN, s), BF16),
            jax.ShapeDtypeStruct((b, s, 3 * D_MLSTM), BF16),
            jax.ShapeDtypeStruct((b, s, D_MLSTM), F32),
            jax.ShapeDtypeStruct((b, s, D_SGU), BF16),
            jax.ShapeDtypeStruct((b, s, 3 * LANES), F32),
            jax.ShapeDtypeStruct((b, LANES, s), F32),
        ],
        scratch_shapes=[pltpu.VMEM((16, LANES), F32)],
        compiler_params=pltpu.CompilerParams(
            dimension_semantics=("arbitrary", "arbitrary"), vmem_limit_bytes=VMEM_LIMIT),
        name="in_projection",
    )(x, sh, sc, *consts)


def _attn_kernel(qT_ref, qTn_ref, k_ref, kb_ref, vT_ref, o_ref, m_ref, bm_ref, acc_ref, s_ref):
    tq = qT_ref.shape[2]
    tk = tq
    qi = pl.program_id(2)

    def score_rhs(qT):
        row = lax.broadcasted_iota(jnp.int32, qT.shape, 0)
        out = []
        for j in range(2):
            qp = jnp.where((row >= j * HEAD_DIM) & (row < (j + 1) * HEAD_DIM), qT, jnp.zeros_like(qT))
            sel = jnp.where((row >= BIAS_PARTS * j) & (row < BIAS_PARTS * (j + 1)), 1.0, 0.0).astype(BF16)
            out.append(jnp.concatenate([qp, sel], axis=0))
        return out

    rhs = score_rhs(qT_ref[0])

    m_ref[...] = jnp.full(m_ref.shape, NEG_BIG, F32)
    acc_ref[...] = jnp.zeros(acc_ref.shape, F32)
    ones = jnp.ones((ONES_ROWS, tk), BF16)

    def scores(kb, j, masked, rhs=rhs):
        off = pl.multiple_of(kb * tk, tk)
        lhs = jnp.concatenate([k_ref[0, pl.ds(off, tk), :], kb_ref[0, pl.ds(off, tk), :]], axis=1)
        s = _dot(lhs, rhs[j])
        if masked:
            kidx = lax.broadcasted_iota(jnp.int32, s.shape, 0)
            qidx = lax.broadcasted_iota(jnp.int32, s.shape, 1)
            s = jnp.where(kidx <= qidx, s, NEG_BIG)
        s_ref[j] = s
        bm_ref[j, 0:1, :] = jnp.max(s, axis=0, keepdims=True)

    def probs(j):
        m_old = m_ref[j, 0:1, :]
        m_new = jnp.maximum(m_old, bm_ref[j, 0:1, :])
        m_ref[j, 0:1, :] = m_new
        return jnp.exp2(s_ref[j] - m_new).astype(BF16), jnp.exp2(m_old - m_new)

    def accumulate(kb, j, p, alpha):
        off = pl.multiple_of(kb * tk, tk)
        vt = jnp.concatenate(
            [vT_ref[0, j * HEAD_DIM:(j + 1) * HEAD_DIM, pl.ds(off, tk)], ones], axis=0)
        acc_ref[j] = alpha * acc_ref[j] + _dot(vt, p)

    def pipelined(kb, next_masked):
        for j in range(2):
            p, alpha = probs(j)
            scores(kb + 1, j, next_masked)
            accumulate(kb, j, p, alpha)

    def finish():
        rhs_next = score_rhs(qTn_ref[0])
        for j in range(2):
            p, alpha = probs(j)
            scores(0, j, False, rhs_next)
            accumulate(qi, j, p, alpha)
        outT = jnp.concatenate(
            [acc_ref[j, 0:HEAD_DIM, :] / acc_ref[j, HEAD_DIM:HEAD_DIM + 1, :] for j in range(2)],
            axis=0)
        o_ref[0] = outT.T.astype(BF16)

    @pl.when(qi == 0)
    def _():
        for j in range(2):
            scores(0, j, True)
        finish()

    @pl.when(qi > 0)
    def _():
        n_plain = qi - 1

        def body(i, carry):
            pipelined(2 * i, False)
            pipelined(2 * i + 1, False)
            return carry

        lax.fori_loop(0, lax.shift_right_logical(n_plain, 1), body, 0)

        @pl.when((n_plain & 1) == 1)
        def _():
            pipelined(n_plain - 1, False)

        pipelined(qi - 1, True)
        finish()


def _fox_attention(qT, k, kb, vT):
    b, s, _ = k.shape
    tq = ATTN_TILE
    pairs = N_ATTN_HEADS // 2
    nq = s // tq
    return pl.pallas_call(
        _attn_kernel,
        grid=(b, pairs, nq),
        in_specs=[
            pl.BlockSpec((1, 2 * HEAD_DIM, tq), lambda bi, p, qi: (bi, p, qi)),
            pl.BlockSpec((1, 2 * HEAD_DIM, tq), lambda bi, p, qi: (bi, p, jnp.minimum(qi + 1, nq - 1))),
            pl.BlockSpec((1, s, 2 * HEAD_DIM), lambda bi, p, qi: (bi, 0, p)),
            pl.BlockSpec((1, s, LANES), lambda bi, p, qi: (bi, 0, p)),
            pl.BlockSpec((1, 2 * HEAD_DIM, s), lambda bi, p, qi: (bi, p, 0)),
        ],
        out_specs=pl.BlockSpec((1, tq, 2 * HEAD_DIM), lambda bi, p, qi: (bi, qi, p)),
        out_shape=jax.ShapeDtypeStruct((b, s, D_ATTN), BF16),
        scratch_shapes=[pltpu.VMEM((2, 8, tq), F32),
                        pltpu.VMEM((2, 8, tq), F32),
                        pltpu.VMEM((2, HEAD_DIM + ONES_ROWS, tq), F32),
                        pltpu.VMEM((2, tq, tq), F32)],
        compiler_params=pltpu.CompilerParams(
            dimension_semantics=("arbitrary", "arbitrary", "arbitrary"),
            vmem_limit_bytes=VMEM_LIMIT),
        name="fox_attention",
    )(qT, qT, k, kb, vT)


def _mlstm_kernel(qkv_ref, osig_ref, gn_ref, gT_ref, e256_ref, gmo_ref, o_ref, cn_ref, acs_ref):
    L = qkv_ref.shape[1]
    dm = D_MLSTM

    @pl.when(pl.program_id(1) == 0)
    def _():
        cn_ref[...] = jnp.zeros_like(cn_ref)
        acs_ref[...] = jnp.zeros_like(acs_ref)

    q = qkv_ref[0, :, 0:dm]
    k = qkv_ref[0, :, dm:2 * dm]
    v = qkv_ref[0, :, 2 * dm:3 * dm]
    a_nat = gn_ref[0, :, 0:LANES]
    A_nat = gn_ref[0, :, LANES:2 * LANES]
    m_nat = gn_ref[0, :, 2 * LANES:3 * LANES]
    aT = gT_ref[0]
    A_prev = acs_ref[0:1, :]
    A_end = A_nat[L - 1:L, :]

    grp = lax.broadcasted_iota(jnp.int32, (L, dm), 1) // HEAD_DIM
    causal = (lax.broadcasted_iota(jnp.int32, (L, L), 0) >= lax.broadcasted_iota(jnp.int32, (L, L), 1))
    qcn = _dot(q, cn_ref[...].astype(BF16))
    zero_q = jnp.zeros_like(q)
    hout = jnp.zeros((L, dm), F32)
    wst = jnp.zeros((L, dm), F32)
    for hd in range(N_MLSTM_HEADS):
        gl = GATE_LANE_M + hd
        a_row = aT[gl:gl + 1, :]
        A_col = A_nat[:, gl:gl + 1]
        w = jnp.exp(jnp.where(causal, a_row - A_col, -jnp.inf))
        sw = _nt_dot(jnp.where(grp == hd, q, zero_q), k) * w
        den_intra = jnp.sum(sw, axis=1, keepdims=True)
        pv = _dot(sw.astype(BF16), v)
        e_col = jnp.exp(A_prev[:, gl:gl + 1] - A_col)
        den = den_intra + e_col * qcn[:, dm + hd:dm + hd + 1]
        r = 1.0 / jnp.maximum(jnp.abs(den), jnp.exp(-m_nat[:, gl:gl + 1]))
        hout = jnp.where(grp == hd, (pv + e_col * qcn[:, 0:dm]) * r, hout)
        wst = jnp.where(grp == hd, jnp.exp(a_nat[:, gl:gl + 1] - A_end[:, gl:gl + 1]), wst)

    msq = _split_dot(hout * hout, e256_ref[...])
    hn = hout * lax.rsqrt(msq + EPS) * gmo_ref[...]
    o_ref[0] = (osig_ref[0] * hn).astype(BF16)

    kw = (k.astype(F32) * wst).astype(BF16)
    rhs = jnp.concatenate([v, jnp.ones((L, LANES), BF16)], axis=1)
    upd = _tn_dot(kw, rhs)
    rowg = lax.broadcasted_iota(jnp.int32, upd.shape, 0) // HEAD_DIM
    col = lax.broadcasted_iota(jnp.int32, upd.shape, 1)
    keep = jnp.where(col < dm, col // HEAD_DIM, col - dm) == rowg
    dvec = jnp.exp(A_prev - A_end)
    decay = jnp.zeros(upd.shape, F32)
    for hd in range(N_MLSTM_HEADS):
        gl = GATE_LANE_M + hd
        decay = jnp.where(rowg == hd, dvec[:, gl:gl + 1], decay)
    cn_ref[...] = decay * cn_ref[...] + jnp.where(keep, upd, 0.0)
    acs_ref[0:1, :] = A_end


def _mlstm(qkvm, osig, gn, gT, e256, gmo):
    b, s, _ = qkvm.shape
    L = MLSTM_CHUNK
    return pl.pallas_call(
        _mlstm_kernel,
        grid=(b, s // L),
        in_specs=[
            pl.BlockSpec((1, L, 3 * D_MLSTM), lambda bi, ci: (bi, ci, 0)),
            pl.BlockSpec((1, L, D_MLSTM), lambda bi, ci: (bi, ci, 0)),
            pl.BlockSpec((1, L, 3 * LANES), lambda bi, ci: (bi, ci, 0)),
            pl.BlockSpec((1, LANES, L), lambda bi, ci: (bi, 0, ci)),
            _const_spec(e256.shape),
            _const_spec(gmo.shape),
        ],
        out_specs=pl.BlockSpec((1, L, D_MLSTM), lambda bi, ci: (bi, ci, 0)),
        out_shape=jax.ShapeDtypeStruct((b, s, D_MLSTM), BF16),
        scratch_shapes=[pltpu.VMEM((D_MLSTM, D_MLSTM + LANES), F32), pltpu.VMEM((8, LANES), F32)],
        compiler_params=pltpu.CompilerParams(
            dimension_semantics=("arbitrary", "arbitrary"), vmem_limit_bytes=VMEM_LIMIT),
        name="mlstm",
    )(qkvm, osig, gn, gT, e256, gmo)


def _post_kernel(x_ref, ya_ref, ym_ref, ys_ref, gt1_ref, sh2_ref, sc2_ref, gt2_ref, g_ref,
                 wo_ref, wup_ref, wconv_ref, bconv_ref, wdown_ref, o_ref, halo_ref, hact_ref):
    tm = x_ref.shape[1]

    @pl.when(pl.program_id(1) == 0)
    def _():
        halo_ref[...] = jnp.zeros_like(halo_ref)

    y = (_dot(ya_ref[0], wo_ref[0:D_ATTN, :])
         + _dot(ym_ref[0], wo_ref[D_ATTN:D_ATTN + D_MLSTM, :])
         + _dot(ys_ref[0], wo_ref[D_ATTN + D_MLSTM:D_MODEL, :]))
    x1 = x_ref[0] + gt1_ref[0] * y
    ms = jnp.mean(x1 * x1, axis=-1, keepdims=True)
    h2 = (x1 * lax.rsqrt(ms + EPS) * g_ref[...] * (1.0 + sc2_ref[0]) + sh2_ref[0]).astype(BF16)

    row = lax.broadcasted_iota(jnp.int32, (tm, FF_CHUNK), 0)

    def conv(cols):
        z = _dot(h2, wup_ref[:, cols])
        prev2 = halo_ref[6:7, cols]
        prev1 = halo_ref[7:8, cols]
        halo_ref[:, cols] = z[tm - 8:tm, :]
        z1 = jnp.where(row == 0, prev1, pltpu.roll(z, 1, axis=0))
        z2 = jnp.where(row == 0, prev2, jnp.where(row == 1, prev1, pltpu.roll(z, 2, axis=0)))
        return (wconv_ref[0:1, cols] * z2 + wconv_ref[1:2, cols] * z1 + wconv_ref[2:3, cols] * z
                + bconv_ref[:, cols])

    for c in range(D_FF // FF_CHUNK):
        a = conv(pl.ds(c * FF_CHUNK, FF_CHUNK))
        gv = conv(pl.ds(D_FF + c * FF_CHUNK, FF_CHUNK))
        hact_ref[:, c * FF_CHUNK:(c + 1) * FF_CHUNK] = (a * jax.nn.sigmoid(a) * gv).astype(BF16)

    o_ref[0] = x1 + gt2_ref[0] * _dot(hact_ref[...], wdown_ref[...])


def _post(x, ya, ym, ys, gt1, sh2, sc2, gt2, g, wo, wup, wconv, bconv, wdown):
    b, s, d = x.shape
    tm = POST_TILE
    tok = lambda width: pl.BlockSpec((1, tm, width), lambda bi, si: (bi, si, 0))
    mod = pl.BlockSpec((1, 1, d), lambda bi, si: (bi, 0, 0))
    consts = [g, wo, wup, wconv, bconv, wdown]
    return pl.pallas_call(
        _post_kernel,
        grid=(b, s // tm),
        in_specs=[tok(d), tok(D_ATTN), tok(D_MLSTM), tok(D_SGU), mod, mod, mod, mod]
        + [_const_spec(c.shape) for c in consts],
        out_specs=tok(d),
        out_shape=jax.ShapeDtypeStruct((b, s, d), F32),
        scratch_shapes=[pltpu.VMEM((8, 2 * D_FF), F32), pltpu.VMEM((tm, D_FF), BF16)],
        compiler_params=pltpu.CompilerParams(
            dimension_semantics=("arbitrary", "arbitrary"), vmem_limit_bytes=VMEM_LIMIT),
        name="out_proj_ffn",
    )(x, ya, ym, ys, gt1, sh2, sc2, gt2, *consts)


def _block_diag_mean(n):
    idx = jnp.arange(n) // HEAD_DIM
    return jnp.where(idx[:, None] == idx[None, :], 1.0 / HEAD_DIM, 0.0).astype(BF16)


def _prep_layer(w_in, b_fox_f, b_m_i, b_m_f, g_q, g_k, g_m_out, g_sgu, w_s, b_s):
    o = 0
    wq = w_in[:, o:o + D_ATTN]; o += D_ATTN
    wk = w_in[:, o:o + D_ATTN]; o += D_ATTN
    wv = w_in[:, o:o + D_ATTN]; o += D_ATTN
    wm = w_in[:, o:o + 4 * D_MLSTM]; o += 4 * D_MLSTM
    ws = w_in[:, o:o + 2 * D_SGU]; o += 2 * D_SGU
    wfa = w_in[:, o:o + N_ATTN_HEADS]; o += N_ATTN_HEADS
    wim = w_in[:, o:o + N_MLSTM_HEADS]; o += N_MLSTM_HEADS
    wfm = w_in[:, o:o + N_MLSTM_HEADS]
    lo, hi = GATE_LANE_M, GATE_LANE_M + N_MLSTM_HEADS
    wg = jnp.zeros((D_MODEL, 2 * LANES), F32)
    wg = wg.at[:, 0:N_ATTN_HEADS].set(wfa).at[:, lo:hi].set(wfm).at[:, LANES + lo:LANES + hi].set(wim)
    gbias = jnp.zeros((1, 2 * LANES), F32)
    gbias = (gbias.at[0, 0:N_ATTN_HEADS].set(b_fox_f).at[0, lo:hi].set(b_m_f)
             .at[0, LANES + lo:LANES + hi].set(b_m_i))
    heads = jnp.arange(N_ATTN_HEADS)
    place = jnp.zeros((BIAS_PARTS * LANES, D_ATTN), F32)
    for r in range(BIAS_PARTS):
        place = place.at[r * LANES + heads, (heads // 2) * LANES + BIAS_PARTS * (heads % 2) + r].set(1.0)
    return dict(
        place=place.astype(BF16),
        wqT=wq.T.astype(BF16), wk=wk.astype(BF16), wvT=wv.T.astype(BF16), wm=wm.astype(BF16),
        ws=ws.astype(BF16), wg=wg.astype(BF16), gbias=gbias,
        gq=g_q.reshape(HEAD_DIM, 1), gk=jnp.tile(g_k, N_ATTN_HEADS).reshape(1, D_ATTN),
        e512=_block_diag_mean(D_ATTN), e256=_block_diag_mean(D_SGU),
        gsgu=g_sgu.reshape(1, D_SGU), wst=w_s.reshape(N_SGU_GROUPS * SGU_CHUNK, SGU_CHUNK),
        bst=jnp.repeat(b_s.T, HEAD_DIM, axis=1), gmo=g_m_out.reshape(1, D_MLSTM),
    )


def kernel(x, c, w_ada, b_ada, g_mix, g_ffn, w_in, b_fox_f, b_m_i, b_m_f, g_q, g_k, g_m_out, g_sgu, w_s, b_s, w_out, w_up, w_conv, b_conv, w_down):
    depth = w_ada.shape[0]
    b, s, d = x.shape
    assert d == D_MODEL and s % max(IN_TILE, POST_TILE, MLSTM_CHUNK, ATTN_TILE) == 0
    mod = _modulation(c, w_ada, b_ada)
    for l in range(depth):
        sh1, sc1, gt1, sh2, sc2, gt2 = [mod[l, :, i * d:(i + 1) * d].reshape(b, 1, d) for i in range(6)]
        w = _prep_layer(w_in[l], b_fox_f[l], b_m_i[l], b_m_f[l], g_q[l], g_k[l], g_m_out[l],
                        g_sgu[l], w_s[l], b_s[l])
        qT, k, kb, vT, qkvm, osig, ysgu, gn, gT = _in_projection(x, sh1, sc1, g_mix[l].reshape(1, d), w)
        ya = _fox_attention(qT, k, kb, vT)
        ym = _mlstm(qkvm, osig, gn, gT, w["e256"], w["gmo"])
        x = _post(x, ya, ym, ysgu, gt1, sh2, sc2, gt2, g_ffn[l].reshape(1, d),
                  w_out[l].astype(BF16), w_up[l].astype(BF16), w_conv[l],
                  b_conv[l].reshape(1, 2 * D_FF), w_down[l].astype(BF16))
    return x
```

```python
import functools

import jax
import jax.numpy as jnp
from jax import lax
from jax.experimental import pallas as pl
from jax.experimental.pallas import tpu as pltpu

F32 = jnp.float32
BF16 = jnp.bfloat16

D_MODEL = 1024
HEAD_DIM = 64
D_ATTN = 512
D_MLSTM = 256
D_SGU = 256
N_ATTN_HEADS = 8
N_MLSTM_HEADS = 4
N_SGU_GROUPS = 4
SGU_CHUNK = 128
D_FF = 2816
EPS = 1e-6
LOG2E = 1.4426950408889634

LANES = 128
IN_TILE = 512
ATTN_TILE = 512
BIAS_PARTS = 3
ONES_ROWS = 16
MLSTM_CHUNK = 512
POST_TILE = 512
FF_CHUNK = 256
GATE_LANE_M = 8
NEG_BIG = -1e30
VMEM_LIMIT = 56 * 1024 * 1024


def _nt_dot(a, b):
    return lax.dot_general(a, b, (((1,), (1,)), ((), ())), preferred_element_type=F32)


def _tn_dot(a, b):
    return lax.dot_general(a, b, (((0,), (0,)), ((), ())), preferred_element_type=F32)


def _dot(a, b):
    return jnp.dot(a, b, preferred_element_type=F32)


def _split_dot(x, e):
    hi = x.astype(BF16)
    lo = (x - hi.astype(F32)).astype(BF16)
    return _dot(hi, e) + _dot(lo, e)


def _scan_rows(x, op, identity):
    n = x.shape[0]
    row = lax.broadcasted_iota(jnp.int32, x.shape, 0)
    k = 1
    while k < n:
        shifted = pltpu.roll(x, k, axis=0)
        x = op(x, jnp.where(row >= k, shifted, identity))
        k *= 2
    return x


def _mod_kernel(c_ref, w_ref, b_ref, o_ref):
    c = c_ref[...]
    ca = c * jax.nn.sigmoid(c)
    o_ref[0] = jnp.dot(ca, w_ref[0], preferred_element_type=F32,
                       precision=lax.Precision.HIGHEST) + b_ref[0]


def _modulation(c, w_ada, b_ada):
    depth, d, n = w_ada.shape
    b = c.shape[0]
    tn = 1536
    return pl.pallas_call(
        _mod_kernel,
        grid=(depth, n // tn),
        in_specs=[
            pl.BlockSpec((b, d), lambda l, j: (0, 0)),
            pl.BlockSpec((1, d, tn), lambda l, j: (l, 0, j)),
            pl.BlockSpec((1, 1, tn), lambda l, j: (l, 0, j)),
        ],
        out_specs=pl.BlockSpec((1, b, tn), lambda l, j: (l, 0, j)),
        out_shape=jax.ShapeDtypeStruct((depth, b, n), F32),
        compiler_params=pltpu.CompilerParams(
            dimension_semantics=("arbitrary", "arbitrary"), vmem_limit_bytes=VMEM_LIMIT),
        name="modulation",
    )(c, w_ada, b_ada.reshape(depth, 1, n))


def _in_kernel(x_ref, sh_ref, sc_ref, g_ref, wqT_ref, wk_ref, wvT_ref, wm_ref, ws_ref, wg_ref,
               gq_ref, gk_ref, e512_ref, e256_ref, gsgu_ref, wst_ref, bst_ref, gbias_ref, place_ref,
               qT_ref, k_ref, kb_ref, vT_ref, qkvm_ref, osig_ref, ysgu_ref, gn_ref, gT_ref,
               carry_ref):
    tm = x_ref.shape[1]

    @pl.when(pl.program_id(1) == 0)
    def _():
        carry_ref[...] = jnp.zeros_like(carry_ref)

    x = x_ref[0]
    ms = jnp.mean(x * x, axis=-1, keepdims=True)
    h = x * lax.rsqrt(ms + EPS) * g_ref[...] * (1.0 + sc_ref[0]) + sh_ref[0]
    hb = h.astype(BF16)

    qT = _nt_dot(wqT_ref[...], hb)
    q_scale = (HEAD_DIM ** -0.5) * LOG2E
    for hd in range(N_ATTN_HEADS):
        blk = qT[hd * HEAD_DIM:(hd + 1) * HEAD_DIM]
        msq = jnp.mean(blk * blk, axis=0, keepdims=True)
        qn = blk * lax.rsqrt(msq + EPS) * (gq_ref[...] * q_scale)
        qT_ref[0, hd * HEAD_DIM:(hd + 1) * HEAD_DIM, :] = qn.astype(BF16)

    kk = _dot(hb, wk_ref[...])
    msk = _dot((kk * kk).astype(BF16), e512_ref[...])
    k_ref[0] = (kk * lax.rsqrt(msk + EPS) * gk_ref[...]).astype(BF16)

    vT_ref[0] = _nt_dot(wvT_ref[...], hb).astype(BF16)

    pm = _dot(hb, wm_ref[...])
    qkvm_ref[0, :, 0:D_MLSTM] = pm[:, 0:D_MLSTM].astype(BF16)
    qkvm_ref[0, :, D_MLSTM:2 * D_MLSTM] = (pm[:, D_MLSTM:2 * D_MLSTM] * HEAD_DIM ** -0.5).astype(BF16)
    qkvm_ref[0, :, 2 * D_MLSTM:3 * D_MLSTM] = pm[:, 2 * D_MLSTM:3 * D_MLSTM].astype(BF16)
    osig_ref[0] = jax.nn.sigmoid(pm[:, 3 * D_MLSTM:4 * D_MLSTM])

    pg = _dot(hb, wg_ref[...]) + gbias_ref[...]
    lane = lax.broadcasted_iota(jnp.int32, (tm, LANES), 1)
    ls = jnp.where(lane < GATE_LANE_M + N_MLSTM_HEADS, jax.nn.log_sigmoid(pg[:, 0:LANES]), 0.0)
    cs = _scan_rows(ls, jnp.add, 0.0) + carry_ref[0:1, :]
    a = pg[:, LANES:2 * LANES] - cs
    cm = jnp.maximum(_scan_rows(a, jnp.maximum, -jnp.inf), carry_ref[8:9, :])
    carry_ref[0:1, :] = cs[tm - 1:tm, :]
    carry_ref[8:9, :] = cm[tm - 1:tm, :]
    nf = -LOG2E * cs
    hi = nf.astype(BF16)
    r1 = nf - hi.astype(F32)
    mid = r1.astype(BF16)
    lo = (r1 - mid.astype(F32)).astype(BF16)
    kb_ref[0] = _dot(jnp.concatenate([hi, mid, lo], axis=1), place_ref[...]).astype(BF16)
    g1 = jnp.where(lane < GATE_LANE_M, nf, a)
    gn_ref[0, :, 0:LANES] = g1
    gn_ref[0, :, LANES:2 * LANES] = cm
    gn_ref[0, :, 2 * LANES:3 * LANES] = cs + cm
    gT_ref[0] = g1.T

    ps = _dot(hb, ws_ref[...])
    gu = jax.nn.gelu(ps[:, 0:D_SGU])
    gv = jax.nn.gelu(ps[:, D_SGU:2 * D_SGU])
    e256 = e256_ref[...]
    xc = gv - _split_dot(gv, e256)
    var = _split_dot(xc * xc, e256)
    vnb = (xc * lax.rsqrt(var + EPS) * gsgu_ref[...]).astype(BF16)
    nrow = N_SGU_GROUPS * SGU_CHUNK
    wrow = lax.broadcasted_iota(jnp.int32, (nrow, SGU_CHUNK), 0)
    wcol = lax.broadcasted_iota(jnp.int32, (nrow, SGU_CHUNK), 1)
    wst = jnp.where((wrow % SGU_CHUNK) >= wcol, wst_ref[...], 0.0).astype(BF16)
    grp = lax.broadcasted_iota(jnp.int32, (SGU_CHUNK, D_SGU), 1) // HEAD_DIM
    for c in range(tm // SGU_CHUNK):
        r = _dot(wst, vnb[c * SGU_CHUNK:(c + 1) * SGU_CHUNK])
        mixed = r[0:SGU_CHUNK]
        for gidx in range(1, N_SGU_GROUPS):
            mixed = jnp.where(grp == gidx, r[gidx * SGU_CHUNK:(gidx + 1) * SGU_CHUNK], mixed)
        ysgu_ref[0, c * SGU_CHUNK:(c + 1) * SGU_CHUNK, :] = (
            gu[c * SGU_CHUNK:(c + 1) * SGU_CHUNK] * (mixed + bst_ref[...])).astype(BF16)


def _const_spec(shape):
    nd = len(shape)
    return pl.BlockSpec(shape, lambda b, s: (0,) * nd, pipeline_mode=pl.Buffered(1))


def _in_projection(x, sh, sc, g, w):
    b, s, d = x.shape
    tm = IN_TILE
    tok = lambda width: pl.BlockSpec((1, tm, width), lambda bi, si: (bi, si, 0))
    tokT = lambda rows: pl.BlockSpec((1, rows, tm), lambda bi, si: (bi, 0, si))
    mod = pl.BlockSpec((1, 1, d), lambda bi, si: (bi, 0, 0))
    consts = [g, w["wqT"], w["wk"], w["wvT"], w["wm"], w["ws"], w["wg"], w["gq"], w["gk"],
              w["e512"], w["e256"], w["gsgu"], w["wst"], w["bst"], w["gbias"], w["place"]]
    return pl.pallas_call(
        _in_kernel,
        grid=(b, s // tm),
        in_specs=[tok(d), mod, mod] + [_const_spec(c.shape) for c in consts],
        out_specs=[tokT(D_ATTN), tok(D_ATTN), tok(D_ATTN), tokT(D_ATTN), tok(3 * D_MLSTM),
                   tok(D_MLSTM), tok(D_SGU), tok(3 * LANES), tokT(LANES)],
        out_shape=[
            jax.ShapeDtypeStruct((b, D_ATTN, s), BF16),
            jax.ShapeDtypeStruct((b, s, D_ATTN), BF16),
            jax.ShapeDtypeStruct((b, s, D_ATTN), BF16),
            jax.ShapeDtypeStruct((b, D_ATTN, s), BF16),
            jax.ShapeDtypeStruct((b, s, 3 * D_MLSTM), BF16),
            jax.ShapeDtypeStruct((b, s, D_MLSTM), F32),
            jax.ShapeDtypeStruct((b, s, D_SGU), BF16),
            jax.ShapeDtypeStruct((b, s, 3 * LANES), F32),
            jax.ShapeDtypeStruct((b, LANES, s), F32),
        ],
        scratch_shapes=[pltpu.VMEM((16, LANES), F32)],
        compiler_params=pltpu.CompilerParams(
            dimension_semantics=("arbitrary", "arbitrary"), vmem_limit_bytes=VMEM_LIMIT),
        name="in_projection",
    )(x, sh, sc, *consts)


def _attn_kernel(qT_ref, qTn_ref, k_ref, kb_ref, vT_ref, o_ref, m_ref, bm_ref, acc_ref, s_ref):
    tq = qT_ref.shape[2]
    tk = tq
    qi = pl.program_id(2)

    def score_rhs(qT):
        row = lax.broadcasted_iota(jnp.int32, qT.shape, 0)
        out = []
        for j in range(2):
            qp = jnp.where((row >= j * HEAD_DIM) & (row < (j + 1) * HEAD_DIM), qT, jnp.zeros_like(qT))
            sel = jnp.where((row >= BIAS_PARTS * j) & (row < BIAS_PARTS * (j + 1)), 1.0, 0.0).astype(BF16)
            out.append(jnp.concatenate([qp, sel], axis=0))
        return out

    rhs = score_rhs(qT_ref[0])

    m_ref[...] = jnp.full(m_ref.shape, NEG_BIG, F32)
    acc_ref[...] = jnp.zeros(acc_ref.shape, F32)
    ones = jnp.ones((ONES_ROWS, tk), BF16)

    def scores(kb, j, masked, rhs=rhs):
        off = pl.multiple_of(kb * tk, tk)
        lhs = jnp.concatenate([k_ref[0, pl.ds(off, tk), :], kb_ref[0, pl.ds(off, tk), :]], axis=1)
        s = _dot(lhs, rhs[j])
        if masked:
            kidx = lax.broadcasted_iota(jnp.int32, s.shape, 0)
            qidx = lax.broadcasted_iota(jnp.int32, s.shape, 1)
            s = jnp.where(kidx <= qidx, s, NEG_BIG)
        s_ref[j] = s
        bm_ref[j, 0:1, :] = jnp.max(s, axis=0, keepdims=True)

    def probs(j):
        m_old = m_ref[j, 0:1, :]
        m_new = jnp.maximum(m_old, bm_ref[j, 0:1, :])
        m_ref[j, 0:1, :] = m_new
        return jnp.exp2(s_ref[j] - m_new).astype(BF16), jnp.exp2(m_old - m_new)

    def accumulate(kb, j, p, alpha):
        off = pl.multiple_of(kb * tk, tk)
        vt = jnp.concatenate(
            [vT_ref[0, j * HEAD_DIM:(j + 1) * HEAD_DIM, pl.ds(off, tk)], ones], axis=0)
        acc_ref[j] = alpha * acc_ref[j] + _dot(vt, p)

    def pipelined(kb, next_masked):
        for j in range(2):
            p, alpha = probs(j)
            scores(kb + 1, j, next_masked)
            accumulate(kb, j, p, alpha)

    def finish():
        rhs_next = score_rhs(qTn_ref[0])
        for j in range(2):
            p, alpha = probs(j)
            scores(0, j, False, rhs_next)
            accumulate(qi, j, p, alpha)
        outT = jnp.concatenate(
            [acc_ref[j, 0:HEAD_DIM, :] / acc_ref[j, HEAD_DIM:HEAD_DIM + 1, :] for j in range(2)],
            axis=0)
        o_ref[0] = outT.T.astype(BF16)

    @pl.when(qi == 0)
    def _():
        for j in range(2):
            scores(0, j, True)
        finish()

    @pl.when(qi > 0)
    def _():
        n_plain = qi - 1

        def body(i, carry):
            for u in range(4):
                pipelined(4 * i + u, False)
            return carry

        lax.fori_loop(0, lax.shift_right_logical(n_plain, 2), body, 0)

        @pl.when((n_plain & 2) == 2)
        def _():
            base = n_plain & ~3
            pipelined(base, False)
            pipelined(base + 1, False)

        @pl.when((n_plain & 1) == 1)
        def _():
            pipelined(n_plain - 1, False)

        pipelined(qi - 1, True)
        finish()


def _fox_attention(qT, k, kb, vT):
    b, s, _ = k.shape
    tq = ATTN_TILE
    pairs = N_ATTN_HEADS // 2
    nq = s // tq
    return pl.pallas_call(
        _attn_kernel,
        grid=(b, pairs, nq),
        in_specs=[
            pl.BlockSpec((1, 2 * HEAD_DIM, tq), lambda bi, p, qi: (bi, p, qi)),
            pl.BlockSpec((1, 2 * HEAD_DIM, tq), lambda bi, p, qi: (bi, p, jnp.minimum(qi + 1, nq - 1))),
            pl.BlockSpec((1, s, 2 * HEAD_DIM), lambda bi, p, qi: (bi, 0, p)),
            pl.BlockSpec((1, s, LANES), lambda bi, p, qi: (bi, 0, p)),
            pl.BlockSpec((1, 2 * HEAD_DIM, s), lambda bi, p, qi: (bi, p, 0)),
        ],
        out_specs=pl.BlockSpec((1, tq, 2 * HEAD_DIM), lambda bi, p, qi: (bi, qi, p)),
        out_shape=jax.ShapeDtypeStruct((b, s, D_ATTN), BF16),
        scratch_shapes=[pltpu.VMEM((2, 8, tq), F32),
                        pltpu.VMEM((2, 8, tq), F32),
                        pltpu.VMEM((2, HEAD_DIM + ONES_ROWS, tq), F32),
                        pltpu.VMEM((2, tq, tq), F32)],
        compiler_params=pltpu.CompilerParams(
            dimension_semantics=("arbitrary", "arbitrary", "arbitrary"),
            vmem_limit_bytes=VMEM_LIMIT),
        name="fox_attention",
    )(qT, qT, k, kb, vT)


def _mlstm_kernel(qkv_ref, osig_ref, gn_ref, gT_ref, e256_ref, gmo_ref, o_ref, cn_ref, acs_ref):
    L = qkv_ref.shape[1]
    dm = D_MLSTM

    @pl.when(pl.program_id(1) == 0)
    def _():
        cn_ref[...] = jnp.zeros_like(cn_ref)
        acs_ref[...] = jnp.zeros_like(acs_ref)

    q = qkv_ref[0, :, 0:dm]
    k = qkv_ref[0, :, dm:2 * dm]
    v = qkv_ref[0, :, 2 * dm:3 * dm]
    a_nat = gn_ref[0, :, 0:LANES]
    A_nat = gn_ref[0, :, LANES:2 * LANES]
    m_nat = gn_ref[0, :, 2 * LANES:3 * LANES]
    aT = gT_ref[0]
    A_prev = acs_ref[0:1, :]
    A_end = A_nat[L - 1:L, :]

    grp = lax.broadcasted_iota(jnp.int32, (L, dm), 1) // HEAD_DIM
    causal = (lax.broadcasted_iota(jnp.int32, (L, L), 0) >= lax.broadcasted_iota(jnp.int32, (L, L), 1))
    qcn = _dot(q, cn_ref[...].astype(BF16))
    zero_q = jnp.zeros_like(q)
    hout = jnp.zeros((L, dm), F32)
    wst = jnp.zeros((L, dm), F32)
    for hd in range(N_MLSTM_HEADS):
        gl = GATE_LANE_M + hd
        a_row = aT[gl:gl + 1, :]
        A_col = A_nat[:, gl:gl + 1]
        w = jnp.exp(jnp.where(causal, a_row - A_col, -jnp.inf))
        sw = _nt_dot(jnp.where(grp == hd, q, zero_q), k) * w
        den_intra = jnp.sum(sw, axis=1, keepdims=True)
        pv = _dot(sw.astype(BF16), v)
        e_col = jnp.exp(A_prev[:, gl:gl + 1] - A_col)
        den = den_intra + e_col * qcn[:, dm + hd:dm + hd + 1]
        r = 1.0 / jnp.maximum(jnp.abs(den), jnp.exp(-m_nat[:, gl:gl + 1]))
        hout = jnp.where(grp == hd, (pv + e_col * qcn[:, 0:dm]) * r, hout)
        wst = jnp.where(grp == hd, jnp.exp(a_nat[:, gl:gl + 1] - A_end[:, gl:gl + 1]), wst)

    msq = _split_dot(hout * hout, e256_ref[...])
    hn = hout * lax.rsqrt(msq + EPS) * gmo_ref[...]
    o_ref[0] = (osig_ref[0] * hn).astype(BF16)

    kw = (k.astype(F32) * wst).astype(BF16)
    rhs = jnp.concatenate([v, jnp.ones((L, LANES), BF16)], axis=1)
    upd = _tn_dot(kw, rhs)
    rowg = lax.broadcasted_iota(jnp.int32, upd.shape, 0) // HEAD_DIM
    col = lax.broadcasted_iota(jnp.int32, upd.shape, 1)
    keep = jnp.where(col < dm, col // HEAD_DIM, col - dm) == rowg
    dvec = jnp.exp(A_prev - A_end)
    decay = jnp.zeros(upd.shape, F32)
    for hd in range(N_MLSTM_HEADS):
        gl = GATE_LANE_M + hd
        decay = jnp.where(rowg == hd, dvec[:, gl:gl + 1], decay)
    cn_ref[...] = decay * cn_ref[...] + jnp.where(keep, upd, 0.0)
    acs_ref[0:1, :] = A_end


def _mlstm(qkvm, osig, gn, gT, e256, gmo):
    b, s, _ = qkvm.shape
    L = MLSTM_CHUNK
    return pl.pallas_call(
        _mlstm_kernel,
        grid=(b, s // L),
        in_specs=[
            pl.BlockSpec((1, L, 3 * D_MLSTM), lambda bi, ci: (bi, ci, 0)),
            pl.BlockSpec((1, L, D_MLSTM), lambda bi, ci: (bi, ci, 0)),
            pl.BlockSpec((1, L, 3 * LANES), lambda bi, ci: (bi, ci, 0)),
            pl.BlockSpec((1, LANES, L), lambda bi, ci: (bi, 0, ci)),
            _const_spec(e256.shape),
            _const_spec(gmo.shape),
        ],
        out_specs=pl.BlockSpec((1, L, D_MLSTM), lambda bi, ci: (bi, ci, 0)),
        out_shape=jax.ShapeDtypeStruct((b, s, D_MLSTM), BF16),
        scratch_shapes=[pltpu.VMEM((D_MLSTM, D_MLSTM + LANES), F32), pltpu.VMEM((8, LANES), F32)],
        compiler_params=pltpu.CompilerParams(
            dimension_semantics=("arbitrary", "arbitrary"), vmem_limit_bytes=VMEM_LIMIT),
        name="mlstm",
    )(qkvm, osig, gn, gT, e256, gmo)


def _post_kernel(x_ref, ya_ref, ym_ref, ys_ref, gt1_ref, sh2_ref, sc2_ref, gt2_ref, g_ref,
                 wo_ref, wup_ref, wconv_ref, bconv_ref, wdown_ref, o_ref, halo_ref, hact_ref):
    tm = x_ref.shape[1]

    @pl.when(pl.program_id(1) == 0)
    def _():
        halo_ref[...] = jnp.zeros_like(halo_ref)

    y = (_dot(ya_ref[0], wo_ref[0:D_ATTN, :])
         + _dot(ym_ref[0], wo_ref[D_ATTN:D_ATTN + D_MLSTM, :])
         + _dot(ys_ref[0], wo_ref[D_ATTN + D_MLSTM:D_MODEL, :]))
    x1 = x_ref[0] + gt1_ref[0] * y
    ms = jnp.mean(x1 * x1, axis=-1, keepdims=True)
    h2 = (x1 * lax.rsqrt(ms + EPS) * g_ref[...] * (1.0 + sc2_ref[0]) + sh2_ref[0]).astype(BF16)

    row = lax.broadcasted_iota(jnp.int32, (tm, FF_CHUNK), 0)

    def conv(cols):
        z = _dot(h2, wup_ref[:, cols])
        prev2 = halo_ref[6:7, cols]
        prev1 = halo_ref[7:8, cols]
        halo_ref[:, cols] = z[tm - 8:tm, :]
        z1 = jnp.where(row == 0, prev1, pltpu.roll(z, 1, axis=0))
        z2 = jnp.where(row == 0, prev2, jnp.where(row == 1, prev1, pltpu.roll(z, 2, axis=0)))
        return (wconv_ref[0:1, cols] * z2 + wconv_ref[1:2, cols] * z1 + wconv_ref[2:3, cols] * z
                + bconv_ref[:, cols])

    for c in range(D_FF // FF_CHUNK):
        a = conv(pl.ds(c * FF_CHUNK, FF_CHUNK))
        gv = conv(pl.ds(D_FF + c * FF_CHUNK, FF_CHUNK))
        hact_ref[:, c * FF_CHUNK:(c + 1) * FF_CHUNK] = (a * jax.nn.sigmoid(a) * gv).astype(BF16)

    o_ref[0] = x1 + gt2_ref[0] * _dot(hact_ref[...], wdown_ref[...])


def _post(x, ya, ym, ys, gt1, sh2, sc2, gt2, g, wo, wup, wconv, bconv, wdown):
    b, s, d = x.shape
    tm = POST_TILE
    tok = lambda width: pl.BlockSpec((1, tm, width), lambda bi, si: (bi, si, 0))
    mod = pl.BlockSpec((1, 1, d), lambda bi, si: (bi, 0, 0))
    consts = [g, wo, wup, wconv, bconv, wdown]
    return pl.pallas_call(
        _post_kernel,
        grid=(b, s // tm),
        in_specs=[tok(d), tok(D_ATTN), tok(D_MLSTM), tok(D_SGU), mod, mod, mod, mod]
        + [_const_spec(c.shape) for c in consts],
        out_specs=tok(d),
        out_shape=jax.ShapeDtypeStruct((b, s, d), F32),
        scratch_shapes=[pltpu.VMEM((8, 2 * D_FF), F32), pltpu.VMEM((tm, D_FF), BF16)],
        compiler_params=pltpu.CompilerParams(
            dimension_semantics=("arbitrary", "arbitrary"), vmem_limit_bytes=VMEM_LIMIT),
        name="out_proj_ffn",
    )(x, ya, ym, ys, gt1, sh2, sc2, gt2, *consts)


def _block_diag_mean(n):
    idx = jnp.arange(n) // HEAD_DIM
    return jnp.where(idx[:, None] == idx[None, :], 1.0 / HEAD_DIM, 0.0).astype(BF16)


def _prep_layer(w_in, b_fox_f, b_m_i, b_m_f, g_q, g_k, g_m_out, g_sgu, w_s, b_s):
    o = 0
    wq = w_in[:, o:o + D_ATTN]; o += D_ATTN
    wk = w_in[:, o:o + D_ATTN]; o += D_ATTN
    wv = w_in[:, o:o + D_ATTN]; o += D_ATTN
    wm = w_in[:, o:o + 4 * D_MLSTM]; o += 4 * D_MLSTM
    ws = w_in[:, o:o + 2 * D_SGU]; o += 2 * D_SGU
    wfa = w_in[:, o:o + N_ATTN_HEADS]; o += N_ATTN_HEADS
    wim = w_in[:, o:o + N_MLSTM_HEADS]; o += N_MLSTM_HEADS
    wfm = w_in[:, o:o + N_MLSTM_HEADS]
    lo, hi = GATE_LANE_M, GATE_LANE_M + N_MLSTM_HEADS
    wg = jnp.zeros((D_MODEL, 2 * LANES), F32)
    wg = wg.at[:, 0:N_ATTN_HEADS].set(wfa).at[:, lo:hi].set(wfm).at[:, LANES + lo:LANES + hi].set(wim)
    gbias = jnp.zeros((1, 2 * LANES), F32)
    gbias = (gbias.at[0, 0:N_ATTN_HEADS].set(b_fox_f).at[0, lo:hi].set(b_m_f)
             .at[0, LANES + lo:LANES + hi].set(b_m_i))
    heads = jnp.arange(N_ATTN_HEADS)
    place = jnp.zeros((BIAS_PARTS * LANES, D_ATTN), F32)
    for r in range(BIAS_PARTS):
        place = place.at[r * LANES + heads, (heads // 2) * LANES + BIAS_PARTS * (heads % 2) + r].set(1.0)
    return dict(
        place=place.astype(BF16),
        wqT=wq.T.astype(BF16), wk=wk.astype(BF16), wvT=wv.T.astype(BF16), wm=wm.astype(BF16),
        ws=ws.astype(BF16), wg=wg.astype(BF16), gbias=gbias,
        gq=g_q.reshape(HEAD_DIM, 1), gk=jnp.tile(g_k, N_ATTN_HEADS).reshape(1, D_ATTN),
        e512=_block_diag_mean(D_ATTN), e256=_block_diag_mean(D_SGU),
        gsgu=g_sgu.reshape(1, D_SGU), wst=w_s.reshape(N_SGU_GROUPS * SGU_CHUNK, SGU_CHUNK),
        bst=jnp.repeat(b_s.T, HEAD_DIM, axis=1), gmo=g_m_out.reshape(1, D_MLSTM),
    )


def kernel(x, c, w_ada, b_ada, g_mix, g_ffn, w_in, b_fox_f, b_m_i, b_m_f, g_q, g_k, g_m_out, g_sgu, w_s, b_s, w_out, w_up, w_conv, b_conv, w_down):
    depth = w_ada.shape[0]
    b, s, d = x.shape
    assert d == D_MODEL and s % max(IN_TILE, POST_TILE, MLSTM_CHUNK, ATTN_TILE) == 0
    mod = _modulation(c, w_ada, b_ada)
    for l in range(depth):
        sh1, sc1, gt1, sh2, sc2, gt2 = [mod[l, :, i * d:(i + 1) * d].reshape(b, 1, d) for i in range(6)]
        w = _prep_layer(w_in[l], b_fox_f[l], b_m_i[l], b_m_f[l], g_q[l], g_k[l], g_m_out[l],
                        g_sgu[l], w_s[l], b_s[l])
        qT, k, kb, vT, qkvm, osig, ysgu, gn, gT = _in_projection(x, sh1, sc1, g_mix[l].reshape(1, d), w)
        ya = _fox_attention(qT, k, kb, vT)
        ym = _mlstm(qkvm, osig, gn, gT, w["e256"], w["gmo"])
        x = _post(x, ya, ym, ysgu, gt1, sh2, sc2, gt2, g_ffn[l].reshape(1, d),
                  w_out[l].astype(BF16), w_up[l].astype(BF16), w_conv[l],
                  b_conv[l].reshape(1, 2 * D_FF), w_down[l].astype(BF16))
    return x
```
